```python
import jax, jax.numpy as jnp
from jax import lax
import numpy as np

D_MODEL = 4096
BATCH = 2
SEQ = 8192
DEPTH = 1

NUM_META = 16
CHUNK = 64
META_PAD = CHUNK - NUM_META

MLSTM_HEADS = 4
MLSTM_DV = 512
MLSTM_DQK = 256
MLSTM_WIDTH = MLSTM_HEADS * MLSTM_DV
GATE_SOFTCAP = 15.0

HGRN_WIDTH = D_MODEL - MLSTM_WIDTH
HGRN_DK = 128
HGRN_HEADS = HGRN_WIDTH // HGRN_DK
HGRN_DV = HGRN_WIDTH // HGRN_HEADS

MIX_WIDTH = MLSTM_WIDTH + HGRN_WIDTH

SPLIT_SIZES = (
    MLSTM_HEADS * MLSTM_DQK,
    MLSTM_HEADS * MLSTM_DQK,
    MLSTM_WIDTH,
    MLSTM_WIDTH,
    MLSTM_HEADS,
    MLSTM_HEADS,
    HGRN_HEADS * HGRN_DK,
    HGRN_HEADS * HGRN_DK,
    HGRN_WIDTH,
    HGRN_WIDTH,
)
IN_WIDTH = sum(SPLIT_SIZES)

N_EXPERTS = 32
TOP_K = 4
D_FF = D_MODEL // 2
SWIGLU_LIMIT = 7.0
SWIGLU_ALPHA = 1.702
MOE_BLOCK = 256

NORM_EPS = 1e-5
HEAD_NORM_EPS = 1e-6

kernel_name = 'hybrid_mlstm_hgrn2_moe'


def _rmsnorm(x, gain):
    xf = x.astype(jnp.float32)
    y = xf * lax.rsqrt(jnp.mean(xf * xf, axis=-1, keepdims=True) + NORM_EPS)
    return (y * gain.astype(jnp.float32)).astype(x.dtype)


def _head_rmsnorm(x, gain):
    h, d = x.shape[1], x.shape[3]
    y = x * lax.rsqrt(jnp.mean(x * x, axis=-1, keepdims=True) + HEAD_NORM_EPS)
    return y * gain.astype(jnp.float32).reshape(h, 1, d)


def _split_points():
    pts, acc = [], 0
    for s in SPLIT_SIZES[:-1]:
        acc += s
        pts.append(acc)
    return pts


def _to_chunks(t):
    b, h, n = t.shape[:3]
    return jnp.moveaxis(t.reshape(b, h, n // CHUNK, CHUNK, *t.shape[3:]), 2, 0)


def _from_chunks(t):
    nc, b, h, l, d = t.shape
    return jnp.moveaxis(t, 0, 2).reshape(b, h, nc * l, d)


def _mlstm_chunkwise(q, k, v, log_i, log_f):
    b, h, n, dk = q.shape
    dv = v.shape[-1]
    causal = jnp.tril(jnp.ones((CHUNK, CHUNK), dtype=bool))

    def step(carry, inp):
        c_state, n_state, m_state = carry
        qc, kc, vc, ic, fc = inp
        a_cum = jnp.cumsum(fc, axis=-1)
        inter = a_cum + m_state[..., None]
        d_log = jnp.where(causal, a_cum[..., :, None] - a_cum[..., None, :] + ic[..., None, :], -jnp.inf)
        m_row = jnp.maximum(inter, d_log.max(axis=-1))
        scores = jnp.einsum('bhld,bhsd->bhls', qc, kc) * jnp.exp(d_log - m_row[..., None])
        inter_w = jnp.exp(inter - m_row)
        num = jnp.einsum('bhls,bhsv->bhlv', scores, vc) + inter_w[..., None] * jnp.einsum('bhld,bhdv->bhlv', qc, c_state)
        den = scores.sum(axis=-1) + inter_w * jnp.einsum('bhld,bhd->bhl', qc, n_state)
        h_out = num / jnp.maximum(jnp.abs(den), jnp.exp(-m_row))[..., None]
        a_end = a_cum[..., -1]
        dec = a_end[..., None] - a_cum + ic
        m_new = jnp.maximum(a_end + m_state, dec.max(axis=-1))
        w = jnp.exp(dec - m_new[..., None])
        carry_w = jnp.exp(a_end + m_state - m_new)
        c_state = carry_w[..., None, None] * c_state + jnp.einsum('bhs,bhsd,bhsv->bhdv', w, kc, vc)
        n_state = carry_w[..., None] * n_state + jnp.einsum('bhs,bhsd->bhd', w, kc)
        return (c_state, n_state, m_new), h_out

    init = (jnp.zeros((b, h, dk, dv), jnp.float32),
            jnp.zeros((b, h, dk), jnp.float32),
            jnp.zeros((b, h), jnp.float32))
    xs = (_to_chunks(q), _to_chunks(k), _to_chunks(v), _to_chunks(log_i), _to_chunks(log_f))
    _, hs = lax.scan(step, init, xs)
    return _from_chunks(hs)


def _hgrn2_chunkwise(q, k, v, log_f):
    b, h, n, dk = q.shape
    dv = v.shape[-1]
    causal = jnp.tril(jnp.ones((CHUNK, CHUNK), dtype=bool))[..., None]

    def step(s_state, inp):
        qc, kc, vc, gc = inp
        g_cum = jnp.cumsum(gc, axis=2)
        diff = g_cum[:, :, :, None, :] - g_cum[:, :, None, :, :]
        decay = jnp.exp(jnp.where(causal, diff, -jnp.inf))
        attn = jnp.einsum('bhlc,bhsc,bhlsc->bhls', qc, kc, decay)
        o = jnp.einsum('bhls,bhsv->bhlv', attn, vc) + jnp.einsum('bhlc,bhcv->bhlv', qc * jnp.exp(g_cum), s_state)
        g_end = g_cum[:, :, -1]
        s_state = jnp.exp(g_end)[..., None] * s_state + jnp.einsum(
            'bhsc,bhsv->bhcv', kc * jnp.exp(g_end[:, :, None] - g_cum), vc)
        return s_state, o

    init = jnp.zeros((b, h, dk, dv), jnp.float32)
    xs = (_to_chunks(q), _to_chunks(k), _to_chunks(v), _to_chunks(log_f))
    _, os_ = lax.scan(step, init, xs)
    return _from_chunks(os_)


def _parallel_mixers(u, w_in_l, b_gates_l, mlstm_gain, lower_bound, hgrn_gain, w_out_l):
    bsz, t_len, _ = u.shape
    n = t_len + META_PAD
    f32 = jnp.float32
    up = jnp.pad(u, ((0, 0), (META_PAD, 0), (0, 0)))
    z = up @ w_in_l
    mq, mk, mv, mo, mi, mf, hq, hf, hi, hg = jnp.split(z, _split_points(), axis=-1)
    is_pad = (jnp.arange(n) < META_PAD)[None, None, :]

    def heads(t, nh):
        return t.reshape(bsz, n, nh, -1).transpose(0, 2, 1, 3).astype(f32)

    def merge(t):
        return t.transpose(0, 2, 1, 3).reshape(bsz, n, -1)

    gates = jnp.concatenate([mi, mf], axis=-1).astype(f32) + b_gates_l.astype(f32)
    gates = (GATE_SOFTCAP * jnp.tanh(gates / GATE_SOFTCAP)).transpose(0, 2, 1)
    log_i = jnp.where(is_pad, -jnp.inf, gates[:, :MLSTM_HEADS])
    log_f = jnp.where(is_pad, 0.0, jax.nn.log_sigmoid(gates[:, MLSTM_HEADS:]))
    hm = _mlstm_chunkwise(heads(mq, MLSTM_HEADS), heads(mk, MLSTM_HEADS) * MLSTM_DQK ** -0.5,
                          heads(mv, MLSTM_HEADS), log_i, log_f)
    hm = merge(_head_rmsnorm(hm, mlstm_gain)) * jax.nn.sigmoid(mo.astype(f32))

    f_pre = heads(hf, HGRN_HEADS)
    lb = lower_bound.astype(f32).reshape(HGRN_HEADS, 1, HGRN_DK)
    log_fh = jnp.logaddexp(jnp.log(lb), jnp.log1p(-lb) + jax.nn.log_sigmoid(f_pre))
    kh = (1.0 - lb) * jax.nn.sigmoid(-f_pre)
    pad4 = is_pad[..., None]
    log_fh = jnp.where(pad4, 0.0, log_fh)
    kh = jnp.where(pad4, 0.0, kh)
    qh = jax.nn.silu(heads(hq, HGRN_HEADS)) * HGRN_DK ** -0.5
    oh = _hgrn2_chunkwise(qh, kh, heads(hi, HGRN_HEADS), log_fh)
    oh = merge(_head_rmsnorm(oh, hgrn_gain)) * jax.nn.silu(hg.astype(f32))

    y = jnp.concatenate([hm, oh], axis=-1).astype(u.dtype) @ w_out_l
    return y[:, META_PAD:]


def _clamped_swiglu(gu):
    gate = jnp.minimum(gu[..., :D_FF], SWIGLU_LIMIT)
    lin = jnp.clip(gu[..., D_FF:], -SWIGLU_LIMIT, SWIGLU_LIMIT)
    return gate * jax.nn.sigmoid(SWIGLU_ALPHA * gate) * (lin + 1.0)


def _moe(x2, layer, w_router, b_router, w_gate_up, b_gate_up, w_down, b_down):
    t_tok, d = x2.shape
    tk = t_tok * TOP_K
    logits = (x2 @ w_router[layer] + b_router[layer]).astype(jnp.float32)
    top_val, top_idx = lax.top_k(logits, TOP_K)
    gate_w = jax.nn.softmax(top_val, axis=-1)
    flat_e = top_idx.reshape(-1)
    flat_tok = jnp.arange(tk, dtype=jnp.int32) // TOP_K
    flat_g = gate_w.reshape(-1).astype(x2.dtype)
    order = jnp.argsort(flat_e)
    e_sorted = flat_e[order]
    counts = jnp.bincount(flat_e, length=N_EXPERTS)
    padded = (counts + MOE_BLOCK - 1) // MOE_BLOCK * MOE_BLOCK
    pad_end = jnp.cumsum(padded)
    pad_start = pad_end - padded
    start = jnp.cumsum(counts) - counts
    dest = pad_start[e_sorted] + (jnp.arange(tk) - start[e_sorted])
    n_blocks = -(-(tk + N_EXPERTS * (MOE_BLOCK - 1)) // MOE_BLOCK)
    p_rows = n_blocks * MOE_BLOCK
    row_tok = jnp.zeros((p_rows,), jnp.int32).at[dest].set(flat_tok[order])
    row_gate = jnp.zeros((p_rows,), x2.dtype).at[dest].set(flat_g[order])
    block_e = jnp.minimum(jnp.searchsorted(pad_end, jnp.arange(n_blocks) * MOE_BLOCK, side='right'),
                          N_EXPERTS - 1)

    def expert_block(args):
        tok, g, e = args
        xb = x2[tok]
        gu = xb @ w_gate_up[layer, e] + b_gate_up[layer, e]
        y = _clamped_swiglu(gu) @ w_down[layer, e] + b_down[layer, e]
        return y * g[:, None]

    ys = lax.map(expert_block, (row_tok.reshape(n_blocks, MOE_BLOCK),
                                row_gate.reshape(n_blocks, MOE_BLOCK), block_e))
    return jax.ops.segment_sum(ys.reshape(p_rows, d), row_tok, num_segments=t_tok)


def setup_inputs(seed: int = 0) -> dict:
    key = jax.random.key(seed)
    ks = jax.random.split(key, 20)
    nrm = jax.random.normal
    f32 = jnp.float32
    x = nrm(ks[0], (BATCH, SEQ, D_MODEL), f32)
    meta_tokens = nrm(ks[1], (NUM_META, D_MODEL), f32)
    norm_mix = 1.0 + 0.02 * nrm(ks[2], (DEPTH, D_MODEL), f32)
    w_in = nrm(ks[3], (DEPTH, D_MODEL, IN_WIDTH), f32) * D_MODEL ** -0.5
    b_i = 0.1 * nrm(ks[4], (DEPTH, MLSTM_HEADS), f32)
    b_f = jnp.linspace(3.0, 6.0, MLSTM_HEADS, dtype=f32)[None] + 0.1 * nrm(ks[5], (DEPTH, MLSTM_HEADS), f32)
    b_mlstm_gates = jnp.concatenate([b_i, b_f], axis=-1)
    mlstm_head_norm = 1.0 + 0.02 * nrm(ks[6], (DEPTH, MLSTM_WIDTH), f32)
    hgrn_lower_bound = 0.1 * nrm(ks[7], (DEPTH + 1, HGRN_WIDTH), f32)
    hgrn_head_norm = 1.0 + 0.02 * nrm(ks[8], (DEPTH, HGRN_WIDTH), f32)
    w_out = nrm(ks[9], (DEPTH, MIX_WIDTH, D_MODEL), f32) * MIX_WIDTH ** -0.5
    norm_ffn = 1.0 + 0.02 * nrm(ks[10], (DEPTH, D_MODEL), f32)
    w_router = nrm(ks[11], (DEPTH, D_MODEL, N_EXPERTS), f32) * D_MODEL ** -0.5
    b_router = 0.01 * nrm(ks[12], (DEPTH, N_EXPERTS), f32)
    w_gate_up = nrm(ks[13], (DEPTH, N_EXPERTS, D_MODEL, 2 * D_FF), f32) * D_MODEL ** -0.5
    b_gate_up = 0.01 * nrm(ks[14], (DEPTH, N_EXPERTS, 2 * D_FF), f32)
    w_down = nrm(ks[15], (DEPTH, N_EXPERTS, D_FF, D_MODEL), f32) * D_FF ** -0.5
    b_down = 0.01 * nrm(ks[16], (DEPTH, N_EXPERTS, D_MODEL), f32)
    norm_final = 1.0 + 0.02 * nrm(ks[17], (D_MODEL,), f32)
    return {'x': x, 'meta_tokens': meta_tokens, 'norm_mix': norm_mix, 'w_in': w_in,
            'b_mlstm_gates': b_mlstm_gates, 'mlstm_head_norm': mlstm_head_norm,
            'hgrn_lower_bound': hgrn_lower_bound, 'hgrn_head_norm': hgrn_head_norm, 'w_out': w_out,
            'norm_ffn': norm_ffn, 'w_router': w_router, 'b_router': b_router, 'w_gate_up': w_gate_up,
            'b_gate_up': b_gate_up, 'w_down': w_down, 'b_down': b_down, 'norm_final': norm_final}


def reference(x, meta_tokens, norm_mix, w_in, b_mlstm_gates, mlstm_head_norm, hgrn_lower_bound,
              hgrn_head_norm, w_out, norm_ffn, w_router, b_router, w_gate_up, b_gate_up, w_down,
              b_down, norm_final):
    bsz = x.shape[0]
    meta = jnp.broadcast_to(meta_tokens[None].astype(x.dtype), (bsz, NUM_META, D_MODEL))
    h = jnp.concatenate([meta, x], axis=1)
    lower_bounds = jnp.cumsum(jax.nn.softmax(hgrn_lower_bound.astype(jnp.float32), axis=0), axis=0)
    for layer in range(DEPTH):
        u = _rmsnorm(h, norm_mix[layer])
        h = h + _parallel_mixers(u, w_in[layer], b_mlstm_gates[layer], mlstm_head_norm[layer],
                                 lower_bounds[layer], hgrn_head_norm[layer], w_out[layer])
        u = _rmsnorm(h, norm_ffn[layer])
        h = h + _moe(u.reshape(-1, D_MODEL), layer, w_router, b_router, w_gate_up, b_gate_up,
                     w_down, b_down).reshape(h.shape)
    h = _rmsnorm(h, norm_final)
    return h[:, NUM_META:]
```

```python
import functools

import jax
import jax.numpy as jnp
from jax import lax
from jax.experimental import pallas as pl
from jax.experimental.pallas import tpu as pltpu

F32 = jnp.float32
BF16 = jnp.bfloat16

NUM_META = 16
MLSTM_DV = 512
MLSTM_DQK = 256
HGRN_DK = 128
GATE_SOFTCAP = 15.0
TOP_K = 4
SWIGLU_LIMIT = 7.0
SWIGLU_ALPHA = 1.702
NORM_EPS = 1e-5
HEAD_NORM_EPS = 1e-6

LANES = 128
VMEM_LIMIT_BYTES = 56 * 1024 * 1024

CHUNK = 256
FRONT_PAD = CHUNK - NUM_META
GATE_LANES = LANES

MOE_TILE = 512


def _largest_tile(total, target):
    best = LANES
    t = LANES
    while t <= min(total, target):
        if total % t == 0:
            best = t
        t += LANES
    assert total % best == 0, (total, target)
    return best


def _log_sigmoid(x):
    return jnp.minimum(x, 0.0) - jnp.log1p(jnp.exp(-jnp.abs(x)))


def _sigmoid(x):
    return 1.0 / (1.0 + jnp.exp(-x))


def _dot(a, b):
    return jnp.dot(a, b, preferred_element_type=F32)


def _dot_nt(a, b):
    return lax.dot_general(a, b, (((1,), (1,)), ((), ())), preferred_element_type=F32)


def _dot_tn(a, b):
    return lax.dot_general(a, b, (((0,), (0,)), ((), ())), preferred_element_type=F32)


def _dot_exact(a, b):
    return jnp.dot(a, b, preferred_element_type=F32, precision=lax.Precision.HIGHEST)


def _params(*sem):
    return pltpu.CompilerParams(dimension_semantics=sem, vmem_limit_bytes=VMEM_LIMIT_BYTES)


def _inproj_kernel(h_ref, gain_ref, wg_ref, w_ref, z_ref, g_ref, u_scr):
    @pl.when(pl.program_id(1) == 0)
    def _():
        x = h_ref[...]
        ms = jnp.mean(x * x, axis=-1, keepdims=True)
        u = (x * lax.rsqrt(ms + NORM_EPS) * gain_ref[...]).astype(BF16)
        u_scr[...] = u
        g_ref[...] = _dot(u, wg_ref[...])

    z_ref[...] = _dot(u_scr[...], w_ref[...]).astype(z_ref.dtype)


def _in_proj(hp, gain, w_main, w_gates):
    m, d = hp.shape
    zw = w_main.shape[1]
    tm = _largest_tile(m, 512)
    tn = _largest_tile(zw, 1024)
    return pl.pallas_call(
        _inproj_kernel,
        grid=(m // tm, zw // tn),
        in_specs=[
            pl.BlockSpec((tm, d), lambda i, j: (i, 0)),
            pl.BlockSpec((1, d), lambda i, j: (0, 0)),
            pl.BlockSpec((d, GATE_LANES), lambda i, j: (0, 0)),
            pl.BlockSpec((d, tn), lambda i, j: (0, j)),
        ],
        out_specs=[
            pl.BlockSpec((tm, tn), lambda i, j: (i, j)),
            pl.BlockSpec((tm, GATE_LANES), lambda i, j: (i, 0)),
        ],
        out_shape=[
            jax.ShapeDtypeStruct((m, zw), BF16),
            jax.ShapeDtypeStruct((m, GATE_LANES), F32),
        ],
        scratch_shapes=[pltpu.VMEM((tm, d), BF16)],
        compiler_params=_params("parallel", "arbitrary"),
        name="in_proj",
    )(hp, gain, w_gates, w_main)


def _mlstm_kernel(q_ref, k_ref, v_ref, o_ref, gc_ref, gr_ref, bc_ref, br_ref, gain_ref, out_ref,
                  ct, nst, mst, *, heads, chunk):
    hd = pl.program_id(1)
    c = pl.program_id(2)
    L = chunk
    neg_inf = -jnp.inf

    @pl.when(c == 0)
    def _():
        ct[...] = jnp.zeros_like(ct)
        nst[...] = jnp.zeros_like(nst)
        mst[...] = jnp.zeros_like(mst)

    pos_col = lax.broadcasted_iota(jnp.int32, (L, 1), 0) + c * L
    pos_row = lax.broadcasted_iota(jnp.int32, (1, L), 1) + c * L
    pad_col = pos_col < FRONT_PAD
    pad_row = pos_row < FRONT_PAD

    gcol = gc_ref[...] + bc_ref[...]
    gcol = GATE_SOFTCAP * jnp.tanh(gcol / GATE_SOFTCAP)
    lane = lax.broadcasted_iota(jnp.int32, gcol.shape, 1)
    lf_col_all = jnp.where(pad_col, 0.0, _log_sigmoid(gcol))
    li_col = jnp.sum(jnp.where(lane == hd, gcol, 0.0), axis=1, keepdims=True)
    li_col = jnp.where(pad_col, neg_inf, li_col)

    grow = gr_ref[...] + br_ref[...]
    grow = GATE_SOFTCAP * jnp.tanh(grow / GATE_SOFTCAP)
    sub = lax.broadcasted_iota(jnp.int32, grow.shape, 0)
    lf_row_all = jnp.where(pad_row, 0.0, _log_sigmoid(grow))
    li_row = jnp.sum(jnp.where(sub == hd, grow, 0.0), axis=0, keepdims=True)
    li_row = jnp.where(pad_row, neg_inf, li_row)

    r_i = lax.broadcasted_iota(jnp.int32, (L, L), 0)
    c_i = lax.broadcasted_iota(jnp.int32, (L, L), 1)
    causal = r_i >= c_i
    lower = causal.astype(F32)
    upper = (r_i <= c_i).astype(F32)
    a_col_all = _dot_exact(lower, lf_col_all)
    a_col = jnp.sum(jnp.where(lane == heads + hd, a_col_all, 0.0), axis=1, keepdims=True)
    a_row_all = _dot_exact(lf_row_all, upper)
    a_row = jnp.sum(jnp.where(sub == heads + hd, a_row_all, 0.0), axis=0, keepdims=True)

    m_prev = mst[...]
    inter = a_col + m_prev
    d_log = jnp.where(causal, a_col - a_row + li_row, neg_inf)
    m_row = jnp.maximum(inter, jnp.max(d_log, axis=1, keepdims=True))

    q = q_ref[...]
    kf = k_ref[...].astype(F32) * (MLSTM_DQK ** -0.5)
    kb = kf.astype(BF16)
    v = v_ref[...]
    scores = _dot_nt(q, kb) * jnp.exp(d_log - m_row)
    inter_w = jnp.exp(inter - m_row)
    num = _dot(scores.astype(BF16), v) + inter_w * _dot_nt(q, ct[...].astype(BF16))
    den = jnp.sum(scores, axis=1, keepdims=True) + inter_w * jnp.sum(
        q.astype(F32) * nst[...], axis=1, keepdims=True)
    h_out = num / jnp.maximum(jnp.abs(den), jnp.exp(-m_row))

    y = h_out * lax.rsqrt(jnp.mean(h_out * h_out, axis=-1, keepdims=True) + HEAD_NORM_EPS)
    y = y * gain_ref[...] * _sigmoid(o_ref[...].astype(F32))
    out_ref[...] = y.astype(out_ref.dtype)

    a_end = a_col[L - 1:L, :]
    dec_col = a_end - a_col + li_col
    dec_row = a_end - a_row + li_row
    m_new = jnp.maximum(a_end + m_prev, jnp.max(dec_row, axis=1, keepdims=True))
    w_col = jnp.exp(dec_col - m_new)
    carry_w = jnp.exp(a_end + m_prev - m_new)
    kw = kf * w_col
    ct[...] = carry_w * ct[...] + _dot_tn(v, kw.astype(BF16))
    nst[...] = carry_w * nst[...] + jnp.sum(kw, axis=0, keepdims=True)
    mst[...] = m_new


def _mlstm(z3, gates_col, gates_row, bias_col, bias_row, gain, heads):
    b, n, _ = z3.shape
    L = CHUNK
    nc = n // L
    dq, dv = MLSTM_DQK, MLSTM_DV
    assert dv == 2 * dq
    kern = functools.partial(_mlstm_kernel, heads=heads, chunk=L)
    return pl.pallas_call(
        kern,
        grid=(b, heads, nc),
        in_specs=[
            pl.BlockSpec((None, L, dq), lambda i, h, c: (i, c, h)),
            pl.BlockSpec((None, L, dq), lambda i, h, c: (i, c, heads + h)),
            pl.BlockSpec((None, L, dv), lambda i, h, c: (i, c, heads + h)),
            pl.BlockSpec((None, L, dv), lambda i, h, c: (i, c, 2 * heads + h)),
            pl.BlockSpec((None, L, GATE_LANES), lambda i, h, c: (i, c, 0)),
            pl.BlockSpec((None, 2 * heads, L), lambda i, h, c: (i, 0, c)),
            pl.BlockSpec((1, GATE_LANES), lambda i, h, c: (0, 0)),
            pl.BlockSpec((2 * heads, 1), lambda i, h, c: (0, 0)),
            pl.BlockSpec((1, dv), lambda i, h, c: (0, h)),
        ],
        out_specs=pl.BlockSpec((None, L, dv), lambda i, h, c: (i, c, h)),
        out_shape=jax.ShapeDtypeStruct((b, n, heads * dv), BF16),
        scratch_shapes=[pltpu.VMEM((dv, dq), F32), pltpu.VMEM((1, dq), F32), pltpu.VMEM((1, 1), F32)],
        compiler_params=_params("parallel", "parallel", "arbitrary"),
        name="mlstm",
    )(z3, z3, z3, z3, gates_col, gates_row, bias_col, bias_row, gain)


def _block_ref_rows(g, hs, L):
    d = g.shape[1]
    if hs >= 4:
        blk = 2 * hs
        g3 = g.reshape(L // blk, blk, d)
        return jnp.broadcast_to(g3[:, hs - 1:hs, :], g3.shape).reshape(L, d)
    row = lax.broadcasted_iota(jnp.int32, (L, 1), 0)
    if hs == 2:
        r = row & 3
        up1 = pltpu.roll(g, L - 1, 0)
        dn1 = pltpu.roll(g, 1, 0)
        dn2 = pltpu.roll(g, 2, 0)
        return jnp.where(r == 0, up1, jnp.where(r == 1, g, jnp.where(r == 2, dn1, dn2)))
    assert hs == 1
    return jnp.where((row & 1) == 1, pltpu.roll(g, 1, 0), g)


def _hgrn_kernel(q_ref, f_ref, i_ref, g_ref, lb_ref, gain_ref, out_ref, st, *, chunk):
    c = pl.program_id(2)
    L = chunk

    @pl.when(c == 0)
    def _():
        st[...] = jnp.zeros_like(st)

    row = lax.broadcasted_iota(jnp.int32, (L, 1), 0)
    is_pad = (row + c * L) < FRONT_PAD

    lb = lb_ref[...]
    f_pre = f_ref[...].astype(F32)
    t0 = jnp.log(lb)
    t1 = jnp.log1p(-lb) + _log_sigmoid(f_pre)
    log_f = jnp.maximum(t0, t1) + jnp.log1p(jnp.exp(-jnp.abs(t0 - t1)))
    kh = (1.0 - lb) * _sigmoid(-f_pre)
    log_f = jnp.where(is_pad, 0.0, log_f)
    kh = jnp.where(is_pad, 0.0, kh)
    qf = q_ref[...].astype(F32)
    qh = qf * _sigmoid(qf) * (HGRN_DK ** -0.5)
    v = i_ref[...]

    r_i = lax.broadcasted_iota(jnp.int32, (L, L), 0)
    c_i = lax.broadcasted_iota(jnp.int32, (L, L), 1)
    lower = (r_i >= c_i).astype(F32)
    g_cum = _dot_exact(lower, log_f)

    o = _dot_nt((qh * jnp.exp(g_cum)).astype(BF16), st[...].astype(BF16))

    diag = jnp.sum(qh * kh, axis=1, keepdims=True)
    attn = jnp.where(r_i == c_i, diag, 0.0)
    hs = L // 2
    while hs >= 1:
        dj = g_cum - _block_ref_rows(g_cum, hs, L)
        e = jnp.exp(-jnp.abs(dj))
        second = (row & hs) != 0
        qj = jnp.where(second, qh * e, 0.0).astype(BF16)
        kj = jnp.where(second, 0.0, kh * e).astype(BF16)
        a = _dot_nt(qj, kj)
        if 2 * hs == L:
            attn = attn + a
        else:
            shift = (2 * hs).bit_length() - 1
            attn = attn + jnp.where((r_i >> shift) == (c_i >> shift), a, 0.0)
        hs //= 2
    o = o + _dot(attn.astype(BF16), v)

    y = o * lax.rsqrt(jnp.mean(o * o, axis=-1, keepdims=True) + HEAD_NORM_EPS)
    gate = g_ref[...].astype(F32)
    y = y * gain_ref[...] * (gate * _sigmoid(gate))
    out_ref[...] = y.astype(out_ref.dtype)

    g_end = g_cum[L - 1:L, :]
    k_dec = (kh * jnp.exp(g_end - g_cum)).astype(BF16)
    st[...] = jnp.exp(g_end) * st[...] + _dot_tn(v, k_dec)


def _hgrn(z3, lb, gain, col0, heads):
    b, n, _ = z3.shape
    L = CHUNK
    nc = n // L
    dk = HGRN_DK
    base = col0 // dk
    kern = functools.partial(_hgrn_kernel, chunk=L)

    def zspec(k):
        return pl.BlockSpec((None, L, dk), lambda i, h, c: (i, c, base + k * heads + h))

    return pl.pallas_call(
        kern,
        grid=(b, heads, nc),
        in_specs=[zspec(0), zspec(1), zspec(2), zspec(3),
                  pl.BlockSpec((1, dk), lambda i, h, c: (0, h)),
                  pl.BlockSpec((1, dk), lambda i, h, c: (0, h))],
        out_specs=pl.BlockSpec((None, L, dk), lambda i, h, c: (i, c, h)),
        out_shape=jax.ShapeDtypeStruct((b, n, heads * dk), BF16),
        scratch_shapes=[pltpu.VMEM((dk, dk), F32)],
        compiler_params=_params("parallel", "parallel", "arbitrary"),
        name="hgrn2",
    )(z3, z3, z3, z3, lb, gain)


def _outproj_kernel(hm_ref, oh_ref, w1_ref, w2_ref, res_ref, gain_ref, wr_ref, br_ref,
                    h1_ref, u2_ref, lg_ref):
    k = pl.program_id(1)

    @pl.when(k == 0)
    def _():
        h1_ref[...] = res_ref[...]

    h1_ref[...] += _dot(hm_ref[...], w1_ref[...]) + _dot(oh_ref[...], w2_ref[...])

    @pl.when(k == pl.num_programs(1) - 1)
    def _():
        h1 = h1_ref[...]
        u2 = h1 * lax.rsqrt(jnp.mean(h1 * h1, axis=-1, keepdims=True) + NORM_EPS) * gain_ref[...]
        u2_ref[...] = u2.astype(u2_ref.dtype)
        lg_ref[...] = _dot_exact(u2, wr_ref[...]) + br_ref[...]


def _out_proj(hm, oh, w_out_b, hp, gain, w_router, b_router):
    m, w1 = hm.shape
    w2 = oh.shape[1]
    d = w_out_b.shape[1]
    assert w1 == w2
    tm = _largest_tile(m, 256)
    tk = _largest_tile(w1, 512)
    nk = w1 // tk
    return pl.pallas_call(
        _outproj_kernel,
        grid=(m // tm, nk),
        in_specs=[
            pl.BlockSpec((tm, tk), lambda i, k: (i, k)),
            pl.BlockSpec((tm, tk), lambda i, k: (i, k)),
            pl.BlockSpec((tk, d), lambda i, k: (k, 0)),
            pl.BlockSpec((tk, d), lambda i, k: (nk + k, 0)),
            pl.BlockSpec((tm, d), lambda i, k: (i, 0)),
            pl.BlockSpec((1, d), lambda i, k: (0, 0)),
            pl.BlockSpec((d, LANES), lambda i, k: (0, 0)),
            pl.BlockSpec((1, LANES), lambda i, k: (0, 0)),
        ],
        out_specs=[
            pl.BlockSpec((tm, d), lambda i, k: (i, 0)),
            pl.BlockSpec((tm, d), lambda i, k: (i, 0)),
            pl.BlockSpec((tm, LANES), lambda i, k: (i, 0)),
        ],
        out_shape=[
            jax.ShapeDtypeStruct((m, d), F32),
            jax.ShapeDtypeStruct((m, d), BF16),
            jax.ShapeDtypeStruct((m, LANES), F32),
        ],
        compiler_params=_params("parallel", "arbitrary"),
        name="out_proj",
    )(hm, oh, w_out_b, w_out_b, hp, gain, w_router, b_router)


def _moe_kernel(te_ref, nu_ref, x_ref, wg_ref, wl_ref, bg_ref, bl_ref, wd_ref, bd_ref, rg_ref,
                y_ref, act, *, n_up, tn_up):
    t = pl.program_id(0)
    j = pl.program_id(1)
    valid = t < nu_ref[0]

    @pl.when(jnp.logical_and(valid, j < n_up))
    def _():
        x = x_ref[...]
        gate = _dot(x, wg_ref[...]) + bg_ref[...]
        lin = _dot(x, wl_ref[...]) + bl_ref[...]
        gate = jnp.minimum(gate, SWIGLU_LIMIT)
        lin = jnp.clip(lin, -SWIGLU_LIMIT, SWIGLU_LIMIT)
        a = gate * _sigmoid(SWIGLU_ALPHA * gate) * (lin + 1.0)
        for jj in range(n_up):
            @pl.when(j == jj)
            def _():
                act[:, jj * tn_up:(jj + 1) * tn_up] = a.astype(act.dtype)

    @pl.when(jnp.logical_and(valid, j >= n_up))
    def _():
        y = _dot(act[...], wd_ref[...]) + bd_ref[...]
        y_ref[...] = (y * rg_ref[...]).astype(y_ref.dtype)


def _moe_experts(tile_e, n_used, x_sorted, w_gu, b_gu, w_dn, b_dn, row_gate):
    p, d = x_sorted.shape
    e, _, two_ff = w_gu.shape
    ff = two_ff // 2
    tm = MOE_TILE
    n_tiles = p // tm
    tn_up = _largest_tile(ff, 512)
    tn_dn = _largest_tile(d, 1024)
    n_up = ff // tn_up
    n_dn = d // tn_dn
    kern = functools.partial(_moe_kernel, n_up=n_up, tn_up=tn_up)

    def tile(t, nu):
        return jnp.minimum(t, nu[0] - 1)

    def up_blk(t, j, nu):
        return jnp.where(t < nu[0], jnp.minimum(j, n_up - 1), n_up - 1)

    def dn_blk(t, j, nu):
        return jnp.where(t < nu[0], jnp.maximum(j - n_up, 0), n_dn - 1)

    grid_spec = pltpu.PrefetchScalarGridSpec(
        num_scalar_prefetch=2,
        grid=(n_tiles, n_up + n_dn),
        in_specs=[
            pl.BlockSpec((tm, d), lambda t, j, te, nu: (tile(t, nu), 0)),
            pl.BlockSpec((None, d, tn_up), lambda t, j, te, nu: (te[tile(t, nu)], 0, up_blk(t, j, nu))),
            pl.BlockSpec((None, d, tn_up),
                         lambda t, j, te, nu: (te[tile(t, nu)], 0, n_up + up_blk(t, j, nu))),
            pl.BlockSpec((None, 1, tn_up), lambda t, j, te, nu: (te[tile(t, nu)], 0, up_blk(t, j, nu))),
            pl.BlockSpec((None, 1, tn_up),
                         lambda t, j, te, nu: (te[tile(t, nu)], 0, n_up + up_blk(t, j, nu))),
            pl.BlockSpec((None, ff, tn_dn), lambda t, j, te, nu: (te[tile(t, nu)], 0, dn_blk(t, j, nu))),
            pl.BlockSpec((None, 1, tn_dn), lambda t, j, te, nu: (te[tile(t, nu)], 0, dn_blk(t, j, nu))),
            pl.BlockSpec((tm, 1), lambda t, j, te, nu: (tile(t, nu), 0)),
        ],
        out_specs=pl.BlockSpec((tm, tn_dn), lambda t, j, te, nu: (tile(t, nu), dn_blk(t, j, nu))),
        scratch_shapes=[pltpu.VMEM((tm, ff), BF16)],
    )
    return pl.pallas_call(
        kern,
        grid_spec=grid_spec,
        out_shape=jax.ShapeDtypeStruct((p, d), BF16),
        compiler_params=_params("arbitrary", "arbitrary"),
        name="moe_experts",
    )(tile_e, n_used, x_sorted, w_gu, w_gu, b_gu, b_gu, w_dn, b_dn, row_gate)


def _final_kernel(h1_ref, y_ref, gain_ref, out_ref):
    h = h1_ref[...]
    for k in range(y_ref.shape[0]):
        h = h + y_ref[k].astype(F32)
    out_ref[...] = h * lax.rsqrt(jnp.mean(h * h, axis=-1, keepdims=True) + NORM_EPS) * gain_ref[...]


def _final(h1_3, yk, gain, seq):
    b, n, d = h1_3.shape
    tm = CHUNK
    skip = (n - seq) // tm
    nblk = seq // tm
    return pl.pallas_call(
        _final_kernel,
        grid=(b, nblk),
        in_specs=[
            pl.BlockSpec((None, tm, d), lambda i, r: (i, skip + r, 0)),
            pl.BlockSpec((TOP_K, tm, d), lambda i, r: (0, i * nblk + r, 0)),
            pl.BlockSpec((1, d), lambda i, r: (0, 0)),
        ],
        out_specs=pl.BlockSpec((None, tm, d), lambda i, r: (i, r, 0)),
        out_shape=jax.ShapeDtypeStruct((b, seq, d), F32),
        compiler_params=_params("parallel", "parallel"),
        name="final_norm",
    )(h1_3, yk, gain)


def kernel(x, meta_tokens, norm_mix, w_in, b_mlstm_gates, mlstm_head_norm, hgrn_lower_bound,
           hgrn_head_norm, w_out, norm_ffn, w_router, b_router, w_gate_up, b_gate_up, w_down,
           b_down, norm_final):
    bsz, seq, d = x.shape
    depth = w_in.shape[0]
    assert depth == 1 and seq % CHUNK == 0
    mh = b_mlstm_gates.shape[-1] // 2
    mw = mh * MLSTM_DV
    hw = hgrn_lower_bound.shape[-1]
    hh = hw // HGRN_DK
    n_exp = w_router.shape[-1]
    n = CHUNK + seq
    t_tok = NUM_META + seq

    meta = jnp.broadcast_to(meta_tokens[None].astype(x.dtype), (bsz, NUM_META, d))
    hp = jnp.concatenate([jnp.zeros((bsz, FRONT_PAD, d), x.dtype), meta, x], axis=1)
    hp2 = hp.reshape(bsz * n, d)

    g0 = 2 * mh * MLSTM_DQK + 2 * mw
    w_in0 = w_in[0]
    w_main = jnp.concatenate([w_in0[:, :g0], w_in0[:, g0 + 2 * mh:]], axis=1).astype(BF16)
    w_gates = jnp.pad(w_in0[:, g0:g0 + 2 * mh], ((0, 0), (0, GATE_LANES - 2 * mh))).astype(BF16)
    bias_col = jnp.pad(b_mlstm_gates[0].astype(F32), (0, GATE_LANES - 2 * mh)).reshape(1, GATE_LANES)
    bias_row = b_mlstm_gates[0].astype(F32).reshape(2 * mh, 1)
    lbs = jnp.cumsum(jax.nn.softmax(hgrn_lower_bound.astype(F32), axis=0), axis=0)[0].reshape(1, hw)

    z, gates = _in_proj(hp2, norm_mix[0].reshape(1, d), w_main, w_gates)
    z3 = z.reshape(bsz, n, -1)
    gates_col = gates.reshape(bsz, n, GATE_LANES)
    gates_row = jnp.swapaxes(gates_col[:, :, :2 * mh], 1, 2)
    hm = _mlstm(z3, gates_col, gates_row, bias_col, bias_row, mlstm_head_norm[0].reshape(1, mw), mh)
    oh = _hgrn(z3, lbs, hgrn_head_norm[0].reshape(1, hw), g0, hh)

    wr = jnp.pad(w_router[0].astype(F32), ((0, 0), (0, LANES - n_exp)))
    br = jnp.pad(b_router[0].astype(F32), (0, LANES - n_exp)).reshape(1, LANES)
    h1, u2, logits = _out_proj(hm.reshape(bsz * n, mw), oh.reshape(bsz * n, hw), w_out[0].astype(BF16),
                               hp2, norm_ffn[0].reshape(1, d), wr, br)

    t_all = bsz * t_tok
    tk = t_all * TOP_K
    logits_tok = logits.reshape(bsz, n, LANES)[:, FRONT_PAD:, :n_exp].reshape(t_all, n_exp)
    top_val, top_idx = lax.top_k(logits_tok, TOP_K)
    gate_w = jax.nn.softmax(top_val, axis=-1)
    flat_e = top_idx.reshape(-1).astype(jnp.int32)
    onehot = (flat_e[:, None] == jnp.arange(n_exp, dtype=jnp.int32)[None, :]).astype(jnp.int32)
    csum = jnp.cumsum(onehot, axis=0)
    rank = jnp.sum((csum - onehot) * onehot, axis=1)
    counts = csum[-1]
    tm = MOE_TILE
    padded = (counts + tm - 1) // tm * tm
    pad_end = jnp.cumsum(padded)
    pad_start = pad_end - padded
    dest = pad_start[flat_e] + rank
    n_tiles = -(-(tk + n_exp * (tm - 1)) // tm)
    p_rows = n_tiles * tm
    tok = jnp.arange(tk, dtype=jnp.int32) // TOP_K
    tok_row = (tok // t_tok) * n + FRONT_PAD + tok % t_tok
    row_src = jnp.zeros((p_rows,), jnp.int32).at[dest].set(tok_row)
    row_gate = jnp.zeros((p_rows,), F32).at[dest].set(gate_w.reshape(-1)).reshape(p_rows, 1)
    tile_e = jnp.minimum(jnp.searchsorted(pad_end, jnp.arange(n_tiles, dtype=jnp.int32) * tm, side='right'),
                         n_exp - 1).astype(jnp.int32)
    n_used = (pad_end[-1] // tm).astype(jnp.int32).reshape(1)

    x_sorted = jnp.take(u2, row_src, axis=0)
    y_sorted = _moe_experts(tile_e, n_used, x_sorted, w_gate_up[0].astype(BF16),
                            b_gate_up[0].astype(F32).reshape(n_exp, 1, -1), w_down[0].astype(BF16),
                            b_down[0].astype(F32).reshape(n_exp, 1, d), row_gate)

    dest_seq = dest.reshape(bsz, t_tok, TOP_K)[:, NUM_META:, :]
    idx = jnp.transpose(dest_seq, (2, 0, 1)).reshape(TOP_K, bsz * seq)
    yk = jnp.take(y_sorted, idx, axis=0)
    return _final(h1.reshape(bsz, n, d), yk, norm_final.reshape(1, d), seq)
```

```python
import functools

import jax
import jax.numpy as jnp
from jax import lax
from jax.experimental import pallas as pl
from jax.experimental.pallas import tpu as pltpu

F32 = jnp.float32
BF16 = jnp.bfloat16

NUM_META = 16
MLSTM_DV = 512
MLSTM_DQK = 256
HGRN_DK = 128
GATE_SOFTCAP = 15.0
TOP_K = 4
SWIGLU_LIMIT = 7.0
SWIGLU_ALPHA = 1.702
NORM_EPS = 1e-5
HEAD_NORM_EPS = 1e-6

LANES = 128
VMEM_LIMIT_BYTES = 56 * 1024 * 1024

CHUNK = 256
FRONT_PAD = CHUNK - NUM_META
GATE_LANES = LANES

MOE_TILE = 512


def _largest_tile(total, target):
    best = LANES
    t = LANES
    while t <= min(total, target):
        if total % t == 0:
            best = t
        t += LANES
    assert total % best == 0, (total, target)
    return best


def _log_sigmoid(x):
    return jnp.minimum(x, 0.0) - jnp.log1p(jnp.exp(-jnp.abs(x)))


def _sigmoid(x):
    return 1.0 / (1.0 + jnp.exp(-x))


def _dot(a, b):
    return jnp.dot(a, b, preferred_element_type=F32)


def _dot_nt(a, b):
    return lax.dot_general(a, b, (((1,), (1,)), ((), ())), preferred_element_type=F32)


def _dot_tn(a, b):
    return lax.dot_general(a, b, (((0,), (0,)), ((), ())), preferred_element_type=F32)


def _dot_exact(a, b):
    return jnp.dot(a, b, preferred_element_type=F32, precision=lax.Precision.HIGHEST)


def _params(*sem):
    return pltpu.CompilerParams(dimension_semantics=sem, vmem_limit_bytes=VMEM_LIMIT_BYTES)


def _inproj_kernel(h_ref, gain_ref, wg_ref, w_ref, z_ref, g_ref, u_scr):
    @pl.when(pl.program_id(1) == 0)
    def _():
        x = h_ref[...]
        ms = jnp.mean(x * x, axis=-1, keepdims=True)
        u = (x * lax.rsqrt(ms + NORM_EPS) * gain_ref[...]).astype(BF16)
        u_scr[...] = u
        g_ref[...] = _dot(u, wg_ref[...])

    z_ref[...] = _dot(u_scr[...], w_ref[...]).astype(z_ref.dtype)


def _in_proj(hp, gain, w_main, w_gates):
    m, d = hp.shape
    zw = w_main.shape[1]
    tm = _largest_tile(m, 512)
    tn = _largest_tile(zw, 1024)
    return pl.pallas_call(
        _inproj_kernel,
        grid=(m // tm, zw // tn),
        in_specs=[
            pl.BlockSpec((tm, d), lambda i, j: (i, 0)),
            pl.BlockSpec((1, d), lambda i, j: (0, 0)),
            pl.BlockSpec((d, GATE_LANES), lambda i, j: (0, 0)),
            pl.BlockSpec((d, tn), lambda i, j: (0, j)),
        ],
        out_specs=[
            pl.BlockSpec((tm, tn), lambda i, j: (i, j)),
            pl.BlockSpec((tm, GATE_LANES), lambda i, j: (i, 0)),
        ],
        out_shape=[
            jax.ShapeDtypeStruct((m, zw), BF16),
            jax.ShapeDtypeStruct((m, GATE_LANES), F32),
        ],
        scratch_shapes=[pltpu.VMEM((tm, d), BF16)],
        compiler_params=_params("parallel", "arbitrary"),
        name="in_proj",
    )(hp, gain, w_gates, w_main)


def _mlstm_kernel(q_ref, k_ref, v_ref, o_ref, gc_ref, gr_ref, bc_ref, br_ref, gain_ref, out_ref,
                  ct, nst, mst, *, heads, chunk):
    hd = pl.program_id(1)
    c = pl.program_id(2)
    L = chunk
    neg_inf = -jnp.inf

    @pl.when(c == 0)
    def _():
        ct[...] = jnp.zeros_like(ct)
        nst[...] = jnp.zeros_like(nst)
        mst[...] = jnp.zeros_like(mst)

    pos_col = lax.broadcasted_iota(jnp.int32, (L, 1), 0) + c * L
    pos_row = lax.broadcasted_iota(jnp.int32, (1, L), 1) + c * L
    pad_col = pos_col < FRONT_PAD
    pad_row = pos_row < FRONT_PAD

    gcol = gc_ref[...] + bc_ref[...]
    gcol = GATE_SOFTCAP * jnp.tanh(gcol / GATE_SOFTCAP)
    lane = lax.broadcasted_iota(jnp.int32, gcol.shape, 1)
    lf_col_all = jnp.where(pad_col, 0.0, _log_sigmoid(gcol))
    li_col = jnp.sum(jnp.where(lane == hd, gcol, 0.0), axis=1, keepdims=True)
    li_col = jnp.where(pad_col, neg_inf, li_col)

    grow = gr_ref[...] + br_ref[...]
    grow = GATE_SOFTCAP * jnp.tanh(grow / GATE_SOFTCAP)
    sub = lax.broadcasted_iota(jnp.int32, grow.shape, 0)
    lf_row_all = jnp.where(pad_row, 0.0, _log_sigmoid(grow))
    li_row = jnp.sum(jnp.where(sub == hd, grow, 0.0), axis=0, keepdims=True)
    li_row = jnp.where(pad_row, neg_inf, li_row)

    r_i = lax.broadcasted_iota(jnp.int32, (L, L), 0)
    c_i = lax.broadcasted_iota(jnp.int32, (L, L), 1)
    causal = r_i >= c_i
    lower = causal.astype(F32)
    upper = (r_i <= c_i).astype(F32)
    a_col_all = _dot_exact(lower, lf_col_all)
    a_col = jnp.sum(jnp.where(lane == heads + hd, a_col_all, 0.0), axis=1, keepdims=True)
    a_row_all = _dot_exact(lf_row_all, upper)
    a_row = jnp.sum(jnp.where(sub == heads + hd, a_row_all, 0.0), axis=0, keepdims=True)

    m_prev = mst[...]
    inter = a_col + m_prev
    d_log = jnp.where(causal, a_col - a_row + li_row, neg_inf)
    m_row = jnp.maximum(inter, jnp.max(d_log, axis=1, keepdims=True))

    q = q_ref[...]
    kf = k_ref[...].astype(F32) * (MLSTM_DQK ** -0.5)
    kb = kf.astype(BF16)
    v = v_ref[...]
    scores = _dot_nt(q, kb) * jnp.exp(d_log - m_row)
    inter_w = jnp.exp(inter - m_row)
    num = _dot(scores.astype(BF16), v) + inter_w * _dot_nt(q, ct[...].astype(BF16))
    den = jnp.sum(scores, axis=1, keepdims=True) + inter_w * jnp.sum(
        q.astype(F32) * nst[...], axis=1, keepdims=True)
    h_out = num / jnp.maximum(jnp.abs(den), jnp.exp(-m_row))

    y = h_out * lax.rsqrt(jnp.mean(h_out * h_out, axis=-1, keepdims=True) + HEAD_NORM_EPS)
    y = y * gain_ref[...] * _sigmoid(o_ref[...].astype(F32))
    out_ref[...] = y.astype(out_ref.dtype)

    a_end = a_col[L - 1:L, :]
    dec_col = a_end - a_col + li_col
    dec_row = a_end - a_row + li_row
    m_new = jnp.maximum(a_end + m_prev, jnp.max(dec_row, axis=1, keepdims=True))
    w_col = jnp.exp(dec_col - m_new)
    carry_w = jnp.exp(a_end + m_prev - m_new)
    kw = kf * w_col
    ct[...] = carry_w * ct[...] + _dot_tn(v, kw.astype(BF16))
    nst[...] = carry_w * nst[...] + jnp.sum(kw, axis=0, keepdims=True)
    mst[...] = m_new


def _mlstm(z3, gates_col, gates_row, bias_col, bias_row, gain, heads):
    b, n, _ = z3.shape
    L = CHUNK
    nc = n // L
    dq, dv = MLSTM_DQK, MLSTM_DV
    assert dv == 2 * dq
    kern = functools.partial(_mlstm_kernel, heads=heads, chunk=L)
    return pl.pallas_call(
        kern,
        grid=(b, heads, nc),
        in_specs=[
            pl.BlockSpec((None, L, dq), lambda i, h, c: (i, c, h)),
            pl.BlockSpec((None, L, dq), lambda i, h, c: (i, c, heads + h)),
            pl.BlockSpec((None, L, dv), lambda i, h, c: (i, c, heads + h)),
            pl.BlockSpec((None, L, dv), lambda i, h, c: (i, c, 2 * heads + h)),
            pl.BlockSpec((None, L, GATE_LANES), lambda i, h, c: (i, c, 0)),
            pl.BlockSpec((None, 2 * heads, L), lambda i, h, c: (i, 0, c)),
            pl.BlockSpec((1, GATE_LANES), lambda i, h, c: (0, 0)),
            pl.BlockSpec((2 * heads, 1), lambda i, h, c: (0, 0)),
            pl.BlockSpec((1, dv), lambda i, h, c: (0, h)),
        ],
        out_specs=pl.BlockSpec((None, L, dv), lambda i, h, c: (i, c, h)),
        out_shape=jax.ShapeDtypeStruct((b, n, heads * dv), BF16),
        scratch_shapes=[pltpu.VMEM((dv, dq), F32), pltpu.VMEM((1, dq), F32), pltpu.VMEM((1, 1), F32)],
        compiler_params=_params("parallel", "parallel", "arbitrary"),
        name="mlstm",
    )(z3, z3, z3, z3, gates_col, gates_row, bias_col, bias_row, gain)


def _block_ref_rows(g, hs, L):
    d = g.shape[1]
    if hs >= 4:
        blk = 2 * hs
        g3 = g.reshape(L // blk, blk, d)
        return jnp.broadcast_to(g3[:, hs - 1:hs, :], g3.shape).reshape(L, d)
    row = lax.broadcasted_iota(jnp.int32, (L, 1), 0)
    if hs == 2:
        r = row & 3
        up1 = pltpu.roll(g, L - 1, 0)
        dn1 = pltpu.roll(g, 1, 0)
        dn2 = pltpu.roll(g, 2, 0)
        return jnp.where(r == 0, up1, jnp.where(r == 1, g, jnp.where(r == 2, dn1, dn2)))
    assert hs == 1
    return jnp.where((row & 1) == 1, pltpu.roll(g, 1, 0), g)


def _hgrn_kernel(q_ref, f_ref, i_ref, g_ref, lb_ref, gain_ref, out_ref, st, *, chunk):
    c = pl.program_id(2)
    L = chunk

    @pl.when(c == 0)
    def _():
        st[...] = jnp.zeros_like(st)

    row = lax.broadcasted_iota(jnp.int32, (L, 1), 0)
    is_pad = (row + c * L) < FRONT_PAD

    lb = lb_ref[...]
    f_pre = f_ref[...].astype(F32)
    t0 = jnp.log(lb)
    t1 = jnp.log1p(-lb) + _log_sigmoid(f_pre)
    log_f = jnp.maximum(t0, t1) + jnp.log1p(jnp.exp(-jnp.abs(t0 - t1)))
    kh = (1.0 - lb) * _sigmoid(-f_pre)
    log_f = jnp.where(is_pad, 0.0, log_f)
    kh = jnp.where(is_pad, 0.0, kh)
    qf = q_ref[...].astype(F32)
    qh = qf * _sigmoid(qf) * (HGRN_DK ** -0.5)
    v = i_ref[...]

    r_i = lax.broadcasted_iota(jnp.int32, (L, L), 0)
    c_i = lax.broadcasted_iota(jnp.int32, (L, L), 1)
    lower = (r_i >= c_i).astype(F32)
    g_cum = _dot_exact(lower, log_f)

    o = _dot_nt((qh * jnp.exp(g_cum)).astype(BF16), st[...].astype(BF16))

    diag = jnp.sum(qh * kh, axis=1, keepdims=True)
    attn = jnp.where(r_i == c_i, diag, 0.0)
    hs = L // 2
    while hs >= 1:
        dj = g_cum - _block_ref_rows(g_cum, hs, L)
        e = jnp.exp(-jnp.abs(dj))
        second = (row & hs) != 0
        qj = jnp.where(second, qh * e, 0.0).astype(BF16)
        kj = jnp.where(second, 0.0, kh * e).astype(BF16)
        a = _dot_nt(qj, kj)
        if 2 * hs == L:
            attn = attn + a
        else:
            shift = (2 * hs).bit_length() - 1
            attn = attn + jnp.where((r_i >> shift) == (c_i >> shift), a, 0.0)
        hs //= 2
    o = o + _dot(attn.astype(BF16), v)

    y = o * lax.rsqrt(jnp.mean(o * o, axis=-1, keepdims=True) + HEAD_NORM_EPS)
    gate = g_ref[...].astype(F32)
    y = y * gain_ref[...] * (gate * _sigmoid(gate))
    out_ref[...] = y.astype(out_ref.dtype)

    g_end = g_cum[L - 1:L, :]
    k_dec = (kh * jnp.exp(g_end - g_cum)).astype(BF16)
    st[...] = jnp.exp(g_end) * st[...] + _dot_tn(v, k_dec)


def _hgrn(z3, lb, gain, col0, heads):
    b, n, _ = z3.shape
    L = CHUNK
    nc = n // L
    dk = HGRN_DK
    base = col0 // dk
    kern = functools.partial(_hgrn_kernel, chunk=L)

    def zspec(k):
        return pl.BlockSpec((None, L, dk), lambda i, h, c: (i, c, base + k * heads + h))

    return pl.pallas_call(
        kern,
        grid=(b, heads, nc),
        in_specs=[zspec(0), zspec(1), zspec(2), zspec(3),
                  pl.BlockSpec((1, dk), lambda i, h, c: (0, h)),
                  pl.BlockSpec((1, dk), lambda i, h, c: (0, h))],
        out_specs=pl.BlockSpec((None, L, dk), lambda i, h, c: (i, c, h)),
        out_shape=jax.ShapeDtypeStruct((b, n, heads * dk), BF16),
        scratch_shapes=[pltpu.VMEM((dk, dk), F32)],
        compiler_params=_params("parallel", "parallel", "arbitrary"),
        name="hgrn2",
    )(z3, z3, z3, z3, lb, gain)


def _outproj_kernel(hm_ref, oh_ref, w_ref, res_ref, gain_ref, wr_ref, br_ref,
                    h1_ref, u2_ref, lg_ref, *, nk):
    k = pl.program_id(1)

    @pl.when(k == 0)
    def _():
        h1_ref[...] = res_ref[...]

    @pl.when(k < nk)
    def _():
        h1_ref[...] += _dot(hm_ref[...], w_ref[...])

    @pl.when(k >= nk)
    def _():
        h1_ref[...] += _dot(oh_ref[...], w_ref[...])

    @pl.when(k == pl.num_programs(1) - 1)
    def _():
        h1 = h1_ref[...]
        u2 = h1 * lax.rsqrt(jnp.mean(h1 * h1, axis=-1, keepdims=True) + NORM_EPS) * gain_ref[...]
        u2_ref[...] = u2.astype(u2_ref.dtype)
        lg_ref[...] = _dot_exact(u2, wr_ref[...]) + br_ref[...]


def _out_proj(hm, oh, w_out_b, hp, gain, w_router, b_router):
    m, w1 = hm.shape
    w2 = oh.shape[1]
    d = w_out_b.shape[1]
    assert w1 == w2
    tm = _largest_tile(m, 384)
    tk = _largest_tile(w1, 512)
    nk = w1 // tk
    return pl.pallas_call(
        functools.partial(_outproj_kernel, nk=nk),
        grid=(m // tm, 2 * nk),
        in_specs=[
            pl.BlockSpec((tm, tk), lambda i, k: (i, jnp.minimum(k, nk - 1))),
            pl.BlockSpec((tm, tk), lambda i, k: (i, jnp.maximum(k - nk, 0))),
            pl.BlockSpec((tk, d), lambda i, k: (k, 0)),
            pl.BlockSpec((tm, d), lambda i, k: (i, 0)),
            pl.BlockSpec((1, d), lambda i, k: (0, 0)),
            pl.BlockSpec((d, LANES), lambda i, k: (0, 0)),
            pl.BlockSpec((1, LANES), lambda i, k: (0, 0)),
        ],
        out_specs=[
            pl.BlockSpec((tm, d), lambda i, k: (i, 0)),
            pl.BlockSpec((tm, d), lambda i, k: (i, 0)),
            pl.BlockSpec((tm, LANES), lambda i, k: (i, 0)),
        ],
        out_shape=[
            jax.ShapeDtypeStruct((m, d), F32),
            jax.ShapeDtypeStruct((m, d), BF16),
            jax.ShapeDtypeStruct((m, LANES), F32),
        ],
        compiler_params=_params("parallel", "arbitrary"),
        name="out_proj",
    )(hm, oh, w_out_b, hp, gain, w_router, b_router)


def _moe_up_kernel(st_ref, se_ref, sj_ref, sf_ref, nu_ref, x_ref, wg_ref, wl_ref, bg_ref, bl_ref,
                   act_ref, wgb, wlb):
    s = pl.program_id(0)
    valid = s < nu_ref[0]

    @pl.when(jnp.logical_and(valid, sf_ref[s] == 1))
    def _():
        wgb[...] = wg_ref[...].astype(BF16)
        wlb[...] = wl_ref[...].astype(BF16)

    @pl.when(valid)
    def _():
        x = x_ref[...]
        gate = _dot(x, wgb[...]) + bg_ref[...]
        lin = _dot(x, wlb[...]) + bl_ref[...]
        gate = jnp.minimum(gate, SWIGLU_LIMIT)
        lin = jnp.clip(lin, -SWIGLU_LIMIT, SWIGLU_LIMIT)
        act_ref[...] = (gate * _sigmoid(SWIGLU_ALPHA * gate) * (lin + 1.0)).astype(act_ref.dtype)


def _moe_down_kernel(st_ref, se_ref, sj_ref, sf_ref, nu_ref, a_ref, wd_ref, bd_ref, rg_ref,
                     y_ref, wdb):
    s = pl.program_id(0)
    valid = s < nu_ref[0]

    @pl.when(jnp.logical_and(valid, sf_ref[s] == 1))
    def _():
        wdb[...] = wd_ref[...].astype(BF16)

    @pl.when(valid)
    def _():
        y = _dot(a_ref[...], wdb[...]) + bd_ref[...]
        y_ref[...] = (y * rg_ref[...]).astype(y_ref.dtype)


def _moe_schedule(tile_e, tile_start, tile_count, n_used, n_blocks, n_tiles):
    s = jnp.arange(n_blocks * n_tiles, dtype=jnp.int32)
    e = tile_e[s // n_blocks]
    local = s - n_blocks * tile_start[e]
    nt = jnp.maximum(tile_count[e], 1)
    sj = jnp.clip(local // nt, 0, n_blocks - 1).astype(jnp.int32)
    st = (tile_start[e] + local % nt).astype(jnp.int32)
    sf = (local % nt == 0).astype(jnp.int32)
    nu = (n_used * n_blocks).astype(jnp.int32).reshape(1)
    return st, e.astype(jnp.int32), sj, sf, nu


def _clamped_step(s, nu):
    return jnp.minimum(s, nu[0] - 1)


def _moe_up(sched, x_sorted, w_gu, b_gu, tn_up):
    p, d = x_sorted.shape
    ff = w_gu.shape[2] // 2
    tm = MOE_TILE
    n_up = ff // tn_up
    n_steps = sched[0].shape[0]
    cl = _clamped_step

    grid_spec = pltpu.PrefetchScalarGridSpec(
        num_scalar_prefetch=5,
        grid=(n_steps,),
        in_specs=[
            pl.BlockSpec((tm, d), lambda s, st, se, sj, sf, nu: (st[cl(s, nu)], 0)),
            pl.BlockSpec((None, d, tn_up), lambda s, st, se, sj, sf, nu: (se[cl(s, nu)], 0, sj[cl(s, nu)])),
            pl.BlockSpec((None, d, tn_up),
                         lambda s, st, se, sj, sf, nu: (se[cl(s, nu)], 0, n_up + sj[cl(s, nu)])),
            pl.BlockSpec((None, 1, tn_up), lambda s, st, se, sj, sf, nu: (se[cl(s, nu)], 0, sj[cl(s, nu)])),
            pl.BlockSpec((None, 1, tn_up),
                         lambda s, st, se, sj, sf, nu: (se[cl(s, nu)], 0, n_up + sj[cl(s, nu)])),
        ],
        out_specs=pl.BlockSpec((tm, tn_up), lambda s, st, se, sj, sf, nu: (st[cl(s, nu)], sj[cl(s, nu)])),
        scratch_shapes=[pltpu.VMEM((d, tn_up), BF16), pltpu.VMEM((d, tn_up), BF16)],
    )
    return pl.pallas_call(
        _moe_up_kernel,
        grid_spec=grid_spec,
        out_shape=jax.ShapeDtypeStruct((p, ff), BF16),
        compiler_params=_params("arbitrary"),
        name="moe_up",
    )(*sched, x_sorted, w_gu, w_gu, b_gu, b_gu)


def _moe_down(sched, act, w_dn, b_dn, row_gate, tn_dn):
    p, ff = act.shape
    d = w_dn.shape[2]
    tm = MOE_TILE
    n_steps = sched[0].shape[0]
    cl = _clamped_step

    grid_spec = pltpu.PrefetchScalarGridSpec(
        num_scalar_prefetch=5,
        grid=(n_steps,),
        in_specs=[
            pl.BlockSpec((tm, ff), lambda s, st, se, sj, sf, nu: (st[cl(s, nu)], 0)),
            pl.BlockSpec((None, ff, tn_dn), lambda s, st, se, sj, sf, nu: (se[cl(s, nu)], 0, sj[cl(s, nu)])),
            pl.BlockSpec((None, 1, tn_dn), lambda s, st, se, sj, sf, nu: (se[cl(s, nu)], 0, sj[cl(s, nu)])),
            pl.BlockSpec((tm, 1), lambda s, st, se, sj, sf, nu: (st[cl(s, nu)], 0)),
        ],
        out_specs=pl.BlockSpec((tm, tn_dn), lambda s, st, se, sj, sf, nu: (st[cl(s, nu)], sj[cl(s, nu)])),
        scratch_shapes=[pltpu.VMEM((ff, tn_dn), BF16)],
    )
    return pl.pallas_call(
        _moe_down_kernel,
        grid_spec=grid_spec,
        out_shape=jax.ShapeDtypeStruct((p, d), BF16),
        compiler_params=_params("arbitrary"),
        name="moe_down",
    )(*sched, act, w_dn, b_dn, row_gate)


def _final_kernel(h1_ref, y_ref, gain_ref, out_ref):
    h = h1_ref[...]
    for k in range(y_ref.shape[0]):
        h = h + y_ref[k].astype(F32)
    out_ref[...] = h * lax.rsqrt(jnp.mean(h * h, axis=-1, keepdims=True) + NORM_EPS) * gain_ref[...]


def _final(h1_3, yk, gain, seq):
    b, n, d = h1_3.shape
    tm = CHUNK
    skip = (n - seq) // tm
    nblk = seq // tm
    return pl.pallas_call(
        _final_kernel,
        grid=(b, nblk),
        in_specs=[
            pl.BlockSpec((None, tm, d), lambda i, r: (i, skip + r, 0)),
            pl.BlockSpec((TOP_K, tm, d), lambda i, r: (0, i * nblk + r, 0)),
            pl.BlockSpec((1, d), lambda i, r: (0, 0)),
        ],
        out_specs=pl.BlockSpec((None, tm, d), lambda i, r: (i, r, 0)),
        out_shape=jax.ShapeDtypeStruct((b, seq, d), F32),
        compiler_params=_params("parallel", "parallel"),
        name="final_norm",
    )(h1_3, yk, gain)


def kernel(x, meta_tokens, norm_mix, w_in, b_mlstm_gates, mlstm_head_norm, hgrn_lower_bound,
           hgrn_head_norm, w_out, norm_ffn, w_router, b_router, w_gate_up, b_gate_up, w_down,
           b_down, norm_final):
    bsz, seq, d = x.shape
    depth = w_in.shape[0]
    assert depth == 1 and seq % CHUNK == 0
    mh = b_mlstm_gates.shape[-1] // 2
    mw = mh * MLSTM_DV
    hw = hgrn_lower_bound.shape[-1]
    hh = hw // HGRN_DK
    n_exp = w_router.shape[-1]
    n = CHUNK + seq
    t_tok = NUM_META + seq

    meta = jnp.broadcast_to(meta_tokens[None].astype(x.dtype), (bsz, NUM_META, d))
    hp = jnp.concatenate([jnp.zeros((bsz, FRONT_PAD, d), x.dtype), meta, x], axis=1)
    hp2 = hp.reshape(bsz * n, d)

    g0 = 2 * mh * MLSTM_DQK + 2 * mw
    w_in0 = w_in[0]
    w_main = jnp.concatenate([w_in0[:, :g0], w_in0[:, g0 + 2 * mh:]], axis=1).astype(BF16)
    w_gates = jnp.pad(w_in0[:, g0:g0 + 2 * mh], ((0, 0), (0, GATE_LANES - 2 * mh))).astype(BF16)
    bias_col = jnp.pad(b_mlstm_gates[0].astype(F32), (0, GATE_LANES - 2 * mh)).reshape(1, GATE_LANES)
    bias_row = b_mlstm_gates[0].astype(F32).reshape(2 * mh, 1)
    lbs = jnp.cumsum(jax.nn.softmax(hgrn_lower_bound.astype(F32), axis=0), axis=0)[0].reshape(1, hw)

    z, gates = _in_proj(hp2, norm_mix[0].reshape(1, d), w_main, w_gates)
    z3 = z.reshape(bsz, n, -1)
    gates_col = gates.reshape(bsz, n, GATE_LANES)
    gates_row = jnp.swapaxes(gates_col[:, :, :2 * mh], 1, 2)
    hm = _mlstm(z3, gates_col, gates_row, bias_col, bias_row, mlstm_head_norm[0].reshape(1, mw), mh)
    oh = _hgrn(z3, lbs, hgrn_head_norm[0].reshape(1, hw), g0, hh)

    wr = jnp.pad(w_router[0].astype(F32), ((0, 0), (0, LANES - n_exp)))
    br = jnp.pad(b_router[0].astype(F32), (0, LANES - n_exp)).reshape(1, LANES)
    h1, u2, logits = _out_proj(hm.reshape(bsz * n, mw), oh.reshape(bsz * n, hw), w_out[0].astype(BF16),
                               hp2, norm_ffn[0].reshape(1, d), wr, br)

    t_all = bsz * t_tok
    tk = t_all * TOP_K
    logits_tok = logits.reshape(bsz, n, LANES)[:, FRONT_PAD:, :n_exp].reshape(t_all, n_exp)
    top_val, top_idx = lax.top_k(logits_tok, TOP_K)
    gate_w = jax.nn.softmax(top_val, axis=-1)
    flat_e = top_idx.reshape(-1).astype(jnp.int32)
    onehot = (flat_e[:, None] == jnp.arange(n_exp, dtype=jnp.int32)[None, :]).astype(jnp.int32)
    csum = jnp.cumsum(onehot, axis=0)
    rank = jnp.sum((csum - onehot) * onehot, axis=1)
    counts = csum[-1]
    tm = MOE_TILE
    padded = (counts + tm - 1) // tm * tm
    pad_end = jnp.cumsum(padded)
    pad_start = pad_end - padded
    dest = pad_start[flat_e] + rank
    n_tiles = -(-(tk + n_exp * (tm - 1)) // tm)
    p_rows = n_tiles * tm
    tok = jnp.arange(tk, dtype=jnp.int32) // TOP_K
    tok_row = (tok // t_tok) * n + FRONT_PAD + tok % t_tok
    tile_e = jnp.minimum(jnp.searchsorted(pad_end, jnp.arange(n_tiles, dtype=jnp.int32) * tm, side='right'),
                         n_exp - 1).astype(jnp.int32)
    n_used = (pad_end[-1] // tm).astype(jnp.int32)
    order = jnp.argsort(flat_e, stable=True).astype(jnp.int32)
    start = jnp.cumsum(counts) - counts
    pos = jnp.arange(p_rows, dtype=jnp.int32)
    e_pos = tile_e[pos // tm]
    idx_in_e = pos - pad_start[e_pos]
    row_valid = idx_in_e < counts[e_pos]
    src = order[jnp.clip(start[e_pos] + idx_in_e, 0, tk - 1)]
    row_src = jnp.where(row_valid, tok_row[src], 0)
    row_gate = jnp.where(row_valid, gate_w.reshape(-1)[src], 0.0).reshape(p_rows, 1)

    ff = w_down.shape[2]
    tn_up = _largest_tile(ff, 512)
    tn_dn = _largest_tile(d, 2048)
    tile_start = (pad_start // tm).astype(jnp.int32)
    tile_count = (padded // tm).astype(jnp.int32)
    sched_up = _moe_schedule(tile_e, tile_start, tile_count, n_used, ff // tn_up, n_tiles)
    sched_dn = _moe_schedule(tile_e, tile_start, tile_count, n_used, d // tn_dn, n_tiles)

    x_sorted = jnp.take(u2, row_src, axis=0)
    act = _moe_up(sched_up, x_sorted, w_gate_up[0], b_gate_up[0].astype(F32).reshape(n_exp, 1, -1), tn_up)
    y_sorted = _moe_down(sched_dn, act, w_down[0], b_down[0].astype(F32).reshape(n_exp, 1, d), row_gate,
                         tn_dn)

    dest_seq = dest.reshape(bsz, t_tok, TOP_K)[:, NUM_META:, :]
    idx = jnp.transpose(dest_seq, (2, 0, 1)).reshape(TOP_K, bsz * seq)
    yk = jnp.take(y_sorted, idx, axis=0)
    return _final(h1.reshape(bsz, n, d), yk, norm_final.reshape(1, d), seq)
```

```python
import functools

import jax
import jax.numpy as jnp
from jax import lax
from jax.experimental import pallas as pl
from jax.experimental.pallas import tpu as pltpu

F32 = jnp.float32
BF16 = jnp.bfloat16
U32 = jnp.uint32

NUM_META = 16
MLSTM_DV = 512
MLSTM_DQK = 256
HGRN_DK = 128
GATE_SOFTCAP = 15.0
TOP_K = 4
SWIGLU_LIMIT = 7.0
SWIGLU_ALPHA = 1.702
NORM_EPS = 1e-5
HEAD_NORM_EPS = 1e-6

LANES = 128
VMEM_LIMIT_BYTES = 56 * 1024 * 1024

CHUNK = 256
FRONT_PAD = CHUNK - NUM_META
GATE_LANES = LANES

MOE_TILE = 512
DISPATCH_WAIT_ROWS = 128


def _largest_tile(total, target):
    best = LANES
    t = LANES
    while t <= min(total, target):
        if total % t == 0:
            best = t
        t += LANES
    assert total % best == 0, (total, target)
    return best


def _log_sigmoid(x):
    return jnp.minimum(x, 0.0) - jnp.log1p(jnp.exp(-jnp.abs(x)))


def _sigmoid(x):
    return 1.0 / (1.0 + jnp.exp(-x))


def _pack_bf16_pair(lo, hi):
    lo_bits = lax.bitcast_convert_type(lo.astype(BF16).astype(F32), U32) >> 16
    hi_bits = lax.bitcast_convert_type(hi.astype(BF16).astype(F32), U32) & jnp.uint32(0xFFFF0000)
    return lo_bits | hi_bits


def _unpack_bf16_pair(w):
    lo = lax.bitcast_convert_type(w << 16, F32)
    hi = lax.bitcast_convert_type(w & jnp.uint32(0xFFFF0000), F32)
    return lo, hi


def _dot(a, b):
    return jnp.dot(a, b, preferred_element_type=F32)


def _dot_nt(a, b):
    return lax.dot_general(a, b, (((1,), (1,)), ((), ())), preferred_element_type=F32)


def _dot_tn(a, b):
    return lax.dot_general(a, b, (((0,), (0,)), ((), ())), preferred_element_type=F32)


def _dot_exact(a, b):
    return jnp.dot(a, b, preferred_element_type=F32, precision=lax.Precision.HIGHEST)


def _params(*sem):
    return pltpu.CompilerParams(dimension_semantics=sem, vmem_limit_bytes=VMEM_LIMIT_BYTES)


def _inproj_kernel(h_ref, gain_ref, wg_ref, w_ref, z_ref, g_ref, u_scr):
    @pl.when(pl.program_id(1) == 0)
    def _():
        x = h_ref[...]
        ms = jnp.mean(x * x, axis=-1, keepdims=True)
        u = (x * lax.rsqrt(ms + NORM_EPS) * gain_ref[...]).astype(BF16)
        u_scr[...] = u
        g_ref[...] = _dot(u, wg_ref[...])

    z_ref[...] = _dot(u_scr[...], w_ref[...]).astype(z_ref.dtype)


def _in_proj(hp, gain, w_main, w_gates):
    m, d = hp.shape
    zw = w_main.shape[1]
    tm = _largest_tile(m, 512)
    tn = _largest_tile(zw, 1024)
    return pl.pallas_call(
        _inproj_kernel,
        grid=(m // tm, zw // tn),
        in_specs=[
            pl.BlockSpec((tm, d), lambda i, j: (i, 0)),
            pl.BlockSpec((1, d), lambda i, j: (0, 0)),
            pl.BlockSpec((d, GATE_LANES), lambda i, j: (0, 0)),
            pl.BlockSpec((d, tn), lambda i, j: (0, j)),
        ],
        out_specs=[
            pl.BlockSpec((tm, tn), lambda i, j: (i, j)),
            pl.BlockSpec((tm, GATE_LANES), lambda i, j: (i, 0)),
        ],
        out_shape=[
            jax.ShapeDtypeStruct((m, zw), BF16),
            jax.ShapeDtypeStruct((m, GATE_LANES), F32),
        ],
        scratch_shapes=[pltpu.VMEM((tm, d), BF16)],
        compiler_params=_params("parallel", "arbitrary"),
        name="in_proj",
    )(hp, gain, w_gates, w_main)


def _mlstm_kernel(q_ref, k_ref, v_ref, o_ref, gc_ref, gr_ref, bc_ref, br_ref, gain_ref, out_ref,
                  ct, nst, mst, *, heads, chunk):
    hd = pl.program_id(1)
    c = pl.program_id(2)
    L = chunk
    neg_inf = -jnp.inf

    @pl.when(c == 0)
    def _():
        ct[...] = jnp.zeros_like(ct)
        nst[...] = jnp.zeros_like(nst)
        mst[...] = jnp.zeros_like(mst)

    pos_col = lax.broadcasted_iota(jnp.int32, (L, 1), 0) + c * L
    pos_row = lax.broadcasted_iota(jnp.int32, (1, L), 1) + c * L
    pad_col = pos_col < FRONT_PAD
    pad_row = pos_row < FRONT_PAD

    gcol = gc_ref[...] + bc_ref[...]
    gcol = GATE_SOFTCAP * jnp.tanh(gcol / GATE_SOFTCAP)
    lane = lax.broadcasted_iota(jnp.int32, gcol.shape, 1)
    lf_col_all = jnp.where(pad_col, 0.0, _log_sigmoid(gcol))
    li_col = jnp.sum(jnp.where(lane == hd, gcol, 0.0), axis=1, keepdims=True)
    li_col = jnp.where(pad_col, neg_inf, li_col)

    grow = gr_ref[...] + br_ref[...]
    grow = GATE_SOFTCAP * jnp.tanh(grow / GATE_SOFTCAP)
    sub = lax.broadcasted_iota(jnp.int32, grow.shape, 0)
    lf_row_all = jnp.where(pad_row, 0.0, _log_sigmoid(grow))
    li_row = jnp.sum(jnp.where(sub == hd, grow, 0.0), axis=0, keepdims=True)
    li_row = jnp.where(pad_row, neg_inf, li_row)

    r_i = lax.broadcasted_iota(jnp.int32, (L, L), 0)
    c_i = lax.broadcasted_iota(jnp.int32, (L, L), 1)
    causal = r_i >= c_i
    lower = causal.astype(F32)
    upper = (r_i <= c_i).astype(F32)
    a_col_all = _dot_exact(lower, lf_col_all)
    a_col = jnp.sum(jnp.where(lane == heads + hd, a_col_all, 0.0), axis=1, keepdims=True)
    a_row_all = _dot_exact(lf_row_all, upper)
    a_row = jnp.sum(jnp.where(sub == heads + hd, a_row_all, 0.0), axis=0, keepdims=True)

    m_prev = mst[...]
    inter = a_col + m_prev
    d_log = jnp.where(causal, a_col - a_row + li_row, neg_inf)
    m_row = jnp.maximum(inter, jnp.max(d_log, axis=1, keepdims=True))

    q = q_ref[...]
    kf = k_ref[...].astype(F32) * (MLSTM_DQK ** -0.5)
    kb = kf.astype(BF16)
    v = v_ref[...]
    scores = _dot_nt(q, kb) * jnp.exp(d_log - m_row)
    inter_w = jnp.exp(inter - m_row)
    num = _dot(scores.astype(BF16), v) + inter_w * _dot_nt(q, ct[...].astype(BF16))
    den = jnp.sum(scores, axis=1, keepdims=True) + inter_w * jnp.sum(
        q.astype(F32) * nst[...], axis=1, keepdims=True)
    h_out = num / jnp.maximum(jnp.abs(den), jnp.exp(-m_row))

    y = h_out * lax.rsqrt(jnp.mean(h_out * h_out, axis=-1, keepdims=True) + HEAD_NORM_EPS)
    y = y * gain_ref[...] * _sigmoid(o_ref[...].astype(F32))
    out_ref[...] = y.astype(out_ref.dtype)

    a_end = a_col[L - 1:L, :]
    dec_col = a_end - a_col + li_col
    dec_row = a_end - a_row + li_row
    m_new = jnp.maximum(a_end + m_prev, jnp.max(dec_row, axis=1, keepdims=True))
    w_col = jnp.exp(dec_col - m_new)
    carry_w = jnp.exp(a_end + m_prev - m_new)
    kw = kf * w_col
    ct[...] = carry_w * ct[...] + _dot_tn(v, kw.astype(BF16))
    nst[...] = carry_w * nst[...] + jnp.sum(kw, axis=0, keepdims=True)
    mst[...] = m_new


def _mlstm(z3, gates_col, gates_row, bias_col, bias_row, gain, heads):
    b, n, _ = z3.shape
    L = CHUNK
    nc = n // L
    dq, dv = MLSTM_DQK, MLSTM_DV
    assert dv == 2 * dq
    kern = functools.partial(_mlstm_kernel, heads=heads, chunk=L)
    return pl.pallas_call(
        kern,
        grid=(b, heads, nc),
        in_specs=[
            pl.BlockSpec((None, L, dq), lambda i, h, c: (i, c, h)),
            pl.BlockSpec((None, L, dq), lambda i, h, c: (i, c, heads + h)),
            pl.BlockSpec((None, L, dv), lambda i, h, c: (i, c, heads + h)),
            pl.BlockSpec((None, L, dv), lambda i, h, c: (i, c, 2 * heads + h)),
            pl.BlockSpec((None, L, GATE_LANES), lambda i, h, c: (i, c, 0)),
            pl.BlockSpec((None, 2 * heads, L), lambda i, h, c: (i, 0, c)),
            pl.BlockSpec((1, GATE_LANES), lambda i, h, c: (0, 0)),
            pl.BlockSpec((2 * heads, 1), lambda i, h, c: (0, 0)),
            pl.BlockSpec((1, dv), lambda i, h, c: (0, h)),
        ],
        out_specs=pl.BlockSpec((None, L, dv), lambda i, h, c: (i, c, h)),
        out_shape=jax.ShapeDtypeStruct((b, n, heads * dv), BF16),
        scratch_shapes=[pltpu.VMEM((dv, dq), F32), pltpu.VMEM((1, dq), F32), pltpu.VMEM((1, 1), F32)],
        compiler_params=_params("parallel", "parallel", "arbitrary"),
        name="mlstm",
    )(z3, z3, z3, z3, gates_col, gates_row, bias_col, bias_row, gain)


def _block_ref_rows(g, hs, L):
    d = g.shape[1]
    if hs >= 4:
        blk = 2 * hs
        g3 = g.reshape(L // blk, blk, d)
        return jnp.broadcast_to(g3[:, hs - 1:hs, :], g3.shape).reshape(L, d)
    row = lax.broadcasted_iota(jnp.int32, (L, 1), 0)
    if hs == 2:
        r = row & 3
        up1 = pltpu.roll(g, L - 1, 0)
        dn1 = pltpu.roll(g, 1, 0)
        dn2 = pltpu.roll(g, 2, 0)
        return jnp.where(r == 0, up1, jnp.where(r == 1, g, jnp.where(r == 2, dn1, dn2)))
    assert hs == 1
    return jnp.where((row & 1) == 1, pltpu.roll(g, 1, 0), g)


def _hgrn_kernel(q_ref, f_ref, i_ref, g_ref, lb_ref, gain_ref, out_ref, st, *, chunk):
    c = pl.program_id(2)
    L = chunk

    @pl.when(c == 0)
    def _():
        st[...] = jnp.zeros_like(st)

    row = lax.broadcasted_iota(jnp.int32, (L, 1), 0)
    is_pad = (row + c * L) < FRONT_PAD

    lb = lb_ref[...]
    f_pre = f_ref[...].astype(F32)
    t0 = jnp.log(lb)
    t1 = jnp.log1p(-lb) + _log_sigmoid(f_pre)
    log_f = jnp.maximum(t0, t1) + jnp.log1p(jnp.exp(-jnp.abs(t0 - t1)))
    kh = (1.0 - lb) * _sigmoid(-f_pre)
    log_f = jnp.where(is_pad, 0.0, log_f)
    kh = jnp.where(is_pad, 0.0, kh)
    qf = q_ref[...].astype(F32)
    qh = qf * _sigmoid(qf) * (HGRN_DK ** -0.5)
    v = i_ref[...]

    r_i = lax.broadcasted_iota(jnp.int32, (L, L), 0)
    c_i = lax.broadcasted_iota(jnp.int32, (L, L), 1)
    lower = (r_i >= c_i).astype(F32)
    g_cum = _dot_exact(lower, log_f)

    o = _dot_nt((qh * jnp.exp(g_cum)).astype(BF16), st[...].astype(BF16))

    diag = jnp.sum(qh * kh, axis=1, keepdims=True)
    attn = jnp.where(r_i == c_i, diag, 0.0)
    hs = L // 2
    while hs >= 1:
        dj = g_cum - _block_ref_rows(g_cum, hs, L)
        e = jnp.exp(-jnp.abs(dj))
        second = (row & hs) != 0
        qj = jnp.where(second, qh * e, 0.0).astype(BF16)
        kj = jnp.where(second, 0.0, kh * e).astype(BF16)
        a = _dot_nt(qj, kj)
        if 2 * hs == L:
            attn = attn + a
        else:
            shift = (2 * hs).bit_length() - 1
            attn = attn + jnp.where((r_i >> shift) == (c_i >> shift), a, 0.0)
        hs //= 2
    o = o + _dot(attn.astype(BF16), v)

    y = o * lax.rsqrt(jnp.mean(o * o, axis=-1, keepdims=True) + HEAD_NORM_EPS)
    gate = g_ref[...].astype(F32)
    y = y * gain_ref[...] * (gate * _sigmoid(gate))
    out_ref[...] = y.astype(out_ref.dtype)

    g_end = g_cum[L - 1:L, :]
    k_dec = (kh * jnp.exp(g_end - g_cum)).astype(BF16)
    st[...] = jnp.exp(g_end) * st[...] + _dot_tn(v, k_dec)


def _hgrn(z3, lb, gain, col0, heads):
    b, n, _ = z3.shape
    L = CHUNK
    nc = n // L
    dk = HGRN_DK
    base = col0 // dk
    kern = functools.partial(_hgrn_kernel, chunk=L)

    def zspec(k):
        return pl.BlockSpec((None, L, dk), lambda i, h, c: (i, c, base + k * heads + h))

    return pl.pallas_call(
        kern,
        grid=(b, heads, nc),
        in_specs=[zspec(0), zspec(1), zspec(2), zspec(3),
                  pl.BlockSpec((1, dk), lambda i, h, c: (0, h)),
                  pl.BlockSpec((1, dk), lambda i, h, c: (0, h))],
        out_specs=pl.BlockSpec((None, L, dk), lambda i, h, c: (i, c, h)),
        out_shape=jax.ShapeDtypeStruct((b, n, heads * dk), BF16),
        scratch_shapes=[pltpu.VMEM((dk, dk), F32)],
        compiler_params=_params("parallel", "parallel", "arbitrary"),
        name="hgrn2",
    )(z3, z3, z3, z3, lb, gain)


def _outproj_kernel(hm_ref, oh_ref, w_ref, res_ref, gain_ref, wr_ref, br_ref,
                    h1_ref, u2_ref, lg_ref, *, nk):
    k = pl.program_id(1)

    @pl.when(k == 0)
    def _():
        h1_ref[...] = res_ref[...]

    @pl.when(k < nk)
    def _():
        h1_ref[...] += _dot(hm_ref[...], w_ref[...])

    @pl.when(k >= nk)
    def _():
        h1_ref[...] += _dot(oh_ref[...], w_ref[...])

    @pl.when(k == pl.num_programs(1) - 1)
    def _():
        h1 = h1_ref[...]
        u2 = h1 * lax.rsqrt(jnp.mean(h1 * h1, axis=-1, keepdims=True) + NORM_EPS) * gain_ref[...]
        half = u2.shape[1] // 2
        u2_ref[...] = _pack_bf16_pair(u2[:, :half], u2[:, half:])
        lg_ref[...] = _dot_exact(u2, wr_ref[...]) + br_ref[...]


def _out_proj(hm, oh, w_out_b, hp, gain, w_router, b_router):
    m, w1 = hm.shape
    w2 = oh.shape[1]
    d = w_out_b.shape[1]
    assert w1 == w2
    tm = _largest_tile(m, 384)
    tk = _largest_tile(w1, 512)
    nk = w1 // tk
    return pl.pallas_call(
        functools.partial(_outproj_kernel, nk=nk),
        grid=(m // tm, 2 * nk),
        in_specs=[
            pl.BlockSpec((tm, tk), lambda i, k: (i, jnp.minimum(k, nk - 1))),
            pl.BlockSpec((tm, tk), lambda i, k: (i, jnp.maximum(k - nk, 0))),
            pl.BlockSpec((tk, d), lambda i, k: (k, 0)),
            pl.BlockSpec((tm, d), lambda i, k: (i, 0)),
            pl.BlockSpec((1, d), lambda i, k: (0, 0)),
            pl.BlockSpec((d, LANES), lambda i, k: (0, 0)),
            pl.BlockSpec((1, LANES), lambda i, k: (0, 0)),
        ],
        out_specs=[
            pl.BlockSpec((tm, d), lambda i, k: (i, 0)),
            pl.BlockSpec((tm, d // 2), lambda i, k: (i, 0)),
            pl.BlockSpec((tm, LANES), lambda i, k: (i, 0)),
        ],
        out_shape=[
            jax.ShapeDtypeStruct((m, d), F32),
            jax.ShapeDtypeStruct((m, d // 2), U32),
            jax.ShapeDtypeStruct((m, LANES), F32),
        ],
        compiler_params=_params("parallel", "arbitrary"),
        name="out_proj",
    )(hm, oh, w_out_b, hp, gain, w_router, b_router)


def _moe_up_kernel(st_ref, se_ref, sj_ref, sf_ref, nu_ref, x_ref, wg_ref, wl_ref, bg_ref, bl_ref,
                   act_ref, wgb, wlb):
    s = pl.program_id(0)
    valid = s < nu_ref[0]

    @pl.when(jnp.logical_and(valid, sf_ref[s] == 1))
    def _():
        wgb[...] = wg_ref[...].astype(BF16)
        wlb[...] = wl_ref[...].astype(BF16)

    @pl.when(valid)
    def _():
        half = wgb.shape[0] // 2
        lo, hi = _unpack_bf16_pair(x_ref[...])
        lo = lo.astype(BF16)
        hi = hi.astype(BF16)
        gate = _dot(lo, wgb[:half, :]) + _dot(hi, wgb[half:, :]) + bg_ref[...]
        lin = _dot(lo, wlb[:half, :]) + _dot(hi, wlb[half:, :]) + bl_ref[...]
        gate = jnp.minimum(gate, SWIGLU_LIMIT)
        lin = jnp.clip(lin, -SWIGLU_LIMIT, SWIGLU_LIMIT)
        act_ref[...] = (gate * _sigmoid(SWIGLU_ALPHA * gate) * (lin + 1.0)).astype(act_ref.dtype)

    @pl.when(jnp.logical_not(valid))
    def _():
        act_ref[...] = jnp.zeros_like(act_ref)


def _moe_down_kernel(st_ref, se_ref, sj_ref, sf_ref, nu_ref, a_ref, wd_ref, bd_ref, y_ref, wdb):
    s = pl.program_id(0)
    valid = s < nu_ref[0]

    @pl.when(jnp.logical_and(valid, sf_ref[s] == 1))
    def _():
        wdb[...] = wd_ref[...].astype(BF16)

    @pl.when(valid)
    def _():
        y = _dot(a_ref[...], wdb[...]) + bd_ref[...]
        half = y.shape[1] // 2
        y_ref[...] = _pack_bf16_pair(y[:, :half], y[:, half:])

    @pl.when(jnp.logical_not(valid))
    def _():
        y_ref[...] = jnp.zeros_like(y_ref)


def _moe_schedule(tile_e, tile_start, tile_count, n_used, n_blocks, n_tiles):
    s = jnp.arange(n_blocks * n_tiles, dtype=jnp.int32)
    used = s < n_used * n_blocks
    e = jnp.repeat(tile_e, n_blocks)
    onehot = e[:, None] == jnp.arange(tile_start.shape[0], dtype=jnp.int32)[None, :]
    start_e = jnp.sum(jnp.where(onehot, tile_start[None, :], 0), axis=1)
    nt = jnp.maximum(jnp.sum(jnp.where(onehot, tile_count[None, :], 0), axis=1), 1)
    local = s - n_blocks * start_e
    sj = jnp.where(used, local // nt, s % n_blocks).astype(jnp.int32)
    st = jnp.where(used, start_e + local % nt, s // n_blocks).astype(jnp.int32)
    sf = jnp.logical_and(used, local % nt == 0).astype(jnp.int32)
    nu = (n_used * n_blocks).astype(jnp.int32).reshape(1)
    return st, e.astype(jnp.int32), sj, sf, nu


def _clamped_step(s, nu):
    return jnp.minimum(s, nu[0] - 1)


def _moe_up(sched, x_sorted, w_gu, b_gu, tn_up):
    p, half = x_sorted.shape
    d = 2 * half
    ff = w_gu.shape[2] // 2
    tm = MOE_TILE
    n_up = ff // tn_up
    n_steps = sched[0].shape[0]
    cl = _clamped_step

    grid_spec = pltpu.PrefetchScalarGridSpec(
        num_scalar_prefetch=5,
        grid=(n_steps,),
        in_specs=[
            pl.BlockSpec((tm, half), lambda s, st, se, sj, sf, nu: (st[cl(s, nu)], 0)),
            pl.BlockSpec((None, d, tn_up), lambda s, st, se, sj, sf, nu: (se[cl(s, nu)], 0, sj[cl(s, nu)])),
            pl.BlockSpec((None, d, tn_up),
                         lambda s, st, se, sj, sf, nu: (se[cl(s, nu)], 0, n_up + sj[cl(s, nu)])),
            pl.BlockSpec((None, 1, tn_up), lambda s, st, se, sj, sf, nu: (se[cl(s, nu)], 0, sj[cl(s, nu)])),
            pl.BlockSpec((None, 1, tn_up),
                         lambda s, st, se, sj, sf, nu: (se[cl(s, nu)], 0, n_up + sj[cl(s, nu)])),
        ],
        out_specs=pl.BlockSpec((tm, tn_up), lambda s, st, se, sj, sf, nu: (st[s], sj[s])),
        scratch_shapes=[pltpu.VMEM((d, tn_up), BF16), pltpu.VMEM((d, tn_up), BF16)],
    )
    return pl.pallas_call(
        _moe_up_kernel,
        grid_spec=grid_spec,
        out_shape=jax.ShapeDtypeStruct((p, ff), BF16),
        compiler_params=_params("arbitrary"),
        name="moe_up",
    )(*sched, x_sorted, w_gu, w_gu, b_gu, b_gu)


def _moe_down(sched, act, w_dn, b_dn, tn_dn):
    p, ff = act.shape
    d = w_dn.shape[2]
    tm = MOE_TILE
    n_steps = sched[0].shape[0]
    cl = _clamped_step

    grid_spec = pltpu.PrefetchScalarGridSpec(
        num_scalar_prefetch=5,
        grid=(n_steps,),
        in_specs=[
            pl.BlockSpec((tm, ff), lambda s, st, se, sj, sf, nu: (st[cl(s, nu)], 0)),
            pl.BlockSpec((None, ff, tn_dn), lambda s, st, se, sj, sf, nu: (se[cl(s, nu)], 0, sj[cl(s, nu)])),
            pl.BlockSpec((None, 1, tn_dn), lambda s, st, se, sj, sf, nu: (se[cl(s, nu)], 0, sj[cl(s, nu)])),
        ],
        out_specs=pl.BlockSpec((tm, tn_dn // 2), lambda s, st, se, sj, sf, nu: (st[s], sj[s])),
        scratch_shapes=[pltpu.VMEM((ff, tn_dn), BF16)],
    )
    return pl.pallas_call(
        _moe_down_kernel,
        grid_spec=grid_spec,
        out_shape=jax.ShapeDtypeStruct((p, d // 2), U32),
        compiler_params=_params("arbitrary"),
        name="moe_down",
    )(*sched, act, w_dn, b_dn)


def _row_copy(src, src_row, dst, dst_row, sem, rows=1):
    return pltpu.make_async_copy(src.at[pl.ds(src_row, rows)], dst.at[pl.ds(dst_row, rows)], sem)


def _dispatch_kernel(dest_ref, zlo_ref, zhi_ref, nused_ref, u2_hbm, xs_hbm, zbuf, sem_tok, sem_zero,
                     *, bsz, n, t_tok, n_exp, n_tiles, tm):
    zbuf[...] = jnp.zeros_like(zbuf)

    def zero_rows(wait):
        def per_expert(e, c):
            def body(p, c2):
                cp = _row_copy(zbuf, 0, xs_hbm, p, sem_zero)
                cp.wait() if wait else cp.start()
                return c2
            return lax.fori_loop(zlo_ref[e], zhi_ref[e], body, c)
        lax.fori_loop(0, n_exp, per_expert, 0)

    def zero_tiles(wait):
        def body(t, c):
            cp = _row_copy(zbuf, 0, xs_hbm, pl.multiple_of(t * tm, tm), sem_zero, rows=tm)
            cp.wait() if wait else cp.start()
            return c
        lax.fori_loop(nused_ref[0], n_tiles, body, 0)

    zero_rows(False)
    zero_tiles(False)

    for b in range(bsz):
        def tok_body(r, c, b=b):
            src_row = b * n + FRONT_PAD + r
            a0 = (b * t_tok + r) * TOP_K
            for k in range(TOP_K):
                _row_copy(u2_hbm, src_row, xs_hbm, dest_ref[a0 + k], sem_tok).start()
            return c
        lax.fori_loop(0, t_tok, tok_body, 0)

    total = bsz * t_tok * TOP_K
    def wait_body(i, c):
        _row_copy(u2_hbm, 0, xs_hbm, 0, sem_tok, rows=DISPATCH_WAIT_ROWS).wait()
        return c
    lax.fori_loop(0, total // DISPATCH_WAIT_ROWS, wait_body, 0)
    for _ in range(total % DISPATCH_WAIT_ROWS):
        _row_copy(u2_hbm, 0, xs_hbm, 0, sem_tok).wait()
    zero_rows(True)
    zero_tiles(True)


def _dispatch(dest, zlo, zhi, n_used, u2p, bsz, n, t_tok, n_tiles):
    half = u2p.shape[1]
    tm = MOE_TILE
    kern = functools.partial(_dispatch_kernel, bsz=bsz, n=n, t_tok=t_tok, n_exp=zlo.shape[0],
                             n_tiles=n_tiles, tm=tm)
    grid_spec = pltpu.PrefetchScalarGridSpec(
        num_scalar_prefetch=4,
        grid=(1,),
        in_specs=[pl.BlockSpec(memory_space=pl.ANY)],
        out_specs=pl.BlockSpec(memory_space=pl.ANY),
        scratch_shapes=[pltpu.VMEM((tm, half), U32), pltpu.SemaphoreType.DMA(()), pltpu.SemaphoreType.DMA(())],
    )
    return pl.pallas_call(
        kern,
        grid_spec=grid_spec,
        out_shape=jax.ShapeDtypeStruct((n_tiles * tm, half), U32),
        compiler_params=_params("arbitrary"),
        name="moe_dispatch",
    )(dest, zlo, zhi, n_used, u2p)


def _final_kernel(dest_ref, h1_ref, g_ref, gain_ref, y_hbm, out_ref, ybuf, sem, *, tm, n_tiles, tn_dn):
    i = pl.program_id(0)

    def gather(tile, slot, wait):
        base = tile * (tm * TOP_K)
        if wait:
            for k in range(TOP_K):
                pltpu.make_async_copy(y_hbm.at[pl.ds(0, tm)], ybuf.at[slot, k], sem.at[slot]).wait()
            return

        def body(r, c):
            for k in range(TOP_K):
                pltpu.make_async_copy(y_hbm.at[pl.ds(dest_ref[base + r * TOP_K + k], 1)],
                                      ybuf.at[slot, k, pl.ds(r, 1)], sem.at[slot]).start()
            return c
        lax.fori_loop(0, tm, body, 0)

    @pl.when(i == 0)
    def _():
        gather(0, 0, False)

    @pl.when(i + 1 < n_tiles)
    def _():
        gather(i + 1, (i + 1) % 2, False)

    slot = i % 2
    gather(i, slot, True)

    h = h1_ref[...]
    g = g_ref[...]
    hw = tn_dn // 2
    for k in range(TOP_K):
        w = ybuf[slot, k]
        parts = []
        for jb in range(w.shape[1] // hw):
            lo, hi = _unpack_bf16_pair(w[:, jb * hw:(jb + 1) * hw])
            parts += [lo, hi]
        h = h + g[:, k:k + 1] * jnp.concatenate(parts, axis=1)
    out_ref[...] = h * lax.rsqrt(jnp.mean(h * h, axis=-1, keepdims=True) + NORM_EPS) * gain_ref[...]


def _final(dest_seq, h1_3, gates_seq, gain, y_sorted, seq, tn_dn):
    b, n, d = h1_3.shape
    tm = CHUNK
    skip = (n - seq) // tm
    nblk = seq // tm
    n_tiles = b * nblk
    kern = functools.partial(_final_kernel, tm=tm, n_tiles=n_tiles, tn_dn=tn_dn)
    grid_spec = pltpu.PrefetchScalarGridSpec(
        num_scalar_prefetch=1,
        grid=(n_tiles,),
        in_specs=[
            pl.BlockSpec((None, tm, d), lambda i, ds: (i // nblk, skip + i % nblk, 0)),
            pl.BlockSpec((tm, TOP_K), lambda i, ds: (i, 0)),
            pl.BlockSpec((1, d), lambda i, ds: (0, 0)),
            pl.BlockSpec(memory_space=pl.ANY),
        ],
        out_specs=pl.BlockSpec((None, tm, d), lambda i, ds: (i // nblk, i % nblk, 0)),
        scratch_shapes=[pltpu.VMEM((2, TOP_K, tm, d // 2), U32), pltpu.SemaphoreType.DMA((2,))],
    )
    return pl.pallas_call(
        kern,
        grid_spec=grid_spec,
        out_shape=jax.ShapeDtypeStruct((b, seq, d), F32),
        compiler_params=_params("arbitrary"),
        name="final_norm",
    )(dest_seq, h1_3, gates_seq, gain, y_sorted)


def kernel(x, meta_tokens, norm_mix, w_in, b_mlstm_gates, mlstm_head_norm, hgrn_lower_bound,
           hgrn_head_norm, w_out, norm_ffn, w_router, b_router, w_gate_up, b_gate_up, w_down,
           b_down, norm_final):
    bsz, seq, d = x.shape
    depth = w_in.shape[0]
    assert depth == 1 and seq % CHUNK == 0
    mh = b_mlstm_gates.shape[-1] // 2
    mw = mh * MLSTM_DV
    hw = hgrn_lower_bound.shape[-1]
    hh = hw // HGRN_DK
    n_exp = w_router.shape[-1]
    n = CHUNK + seq
    t_tok = NUM_META + seq

    meta = jnp.broadcast_to(meta_tokens[None].astype(x.dtype), (bsz, NUM_META, d))
    hp = jnp.concatenate([jnp.zeros((bsz, FRONT_PAD, d), x.dtype), meta, x], axis=1)
    hp2 = hp.reshape(bsz * n, d)

    g0 = 2 * mh * MLSTM_DQK + 2 * mw
    w_in0 = w_in[0]
    w_main = jnp.concatenate([w_in0[:, :g0], w_in0[:, g0 + 2 * mh:]], axis=1).astype(BF16)
    w_gates = jnp.pad(w_in0[:, g0:g0 + 2 * mh], ((0, 0), (0, GATE_LANES - 2 * mh))).astype(BF16)
    bias_col = jnp.pad(b_mlstm_gates[0].astype(F32), (0, GATE_LANES - 2 * mh)).reshape(1, GATE_LANES)
    bias_row = b_mlstm_gates[0].astype(F32).reshape(2 * mh, 1)
    lbs = jnp.cumsum(jax.nn.softmax(hgrn_lower_bound.astype(F32), axis=0), axis=0)[0].reshape(1, hw)

    z, gates = _in_proj(hp2, norm_mix[0].reshape(1, d), w_main, w_gates)
    z3 = z.reshape(bsz, n, -1)
    gates_col = gates.reshape(bsz, n, GATE_LANES)
    gates_row = jnp.swapaxes(gates_col[:, :, :2 * mh], 1, 2)
    hm = _mlstm(z3, gates_col, gates_row, bias_col, bias_row, mlstm_head_norm[0].reshape(1, mw), mh)
    oh = _hgrn(z3, lbs, hgrn_head_norm[0].reshape(1, hw), g0, hh)

    wr = jnp.pad(w_router[0].astype(F32), ((0, 0), (0, LANES - n_exp)))
    br = jnp.pad(b_router[0].astype(F32), (0, LANES - n_exp)).reshape(1, LANES)
    h1, u2, logits = _out_proj(hm.reshape(bsz * n, mw), oh.reshape(bsz * n, hw), w_out[0].astype(BF16),
                               hp2, norm_ffn[0].reshape(1, d), wr, br)

    t_all = bsz * t_tok
    tk = t_all * TOP_K
    logits_tok = logits.reshape(bsz, n, LANES)[:, FRONT_PAD:, :n_exp].reshape(t_all, n_exp)
    top_val, top_idx = lax.top_k(logits_tok, TOP_K)
    gate_w = jax.nn.softmax(top_val, axis=-1)
    flat_e = top_idx.reshape(-1).astype(jnp.int32)
    onehot = (flat_e[:, None] == jnp.arange(n_exp, dtype=jnp.int32)[None, :]).astype(jnp.int32)
    csum = jnp.cumsum(onehot, axis=0)
    rank = jnp.sum((csum - onehot) * onehot, axis=1)
    counts = csum[-1]
    tm = MOE_TILE
    padded = (counts + tm - 1) // tm * tm
    pad_end = jnp.cumsum(padded)
    pad_start = pad_end - padded
    dest = (jnp.sum(onehot * pad_start[None, :], axis=1) + rank).astype(jnp.int32)
    n_tiles = -(-(tk + n_exp * (tm - 1)) // tm)
    tile_row0 = jnp.arange(n_tiles, dtype=jnp.int32) * tm
    tile_e = jnp.minimum(jnp.sum((tile_row0[:, None] >= pad_end[None, :]).astype(jnp.int32), axis=1),
                         n_exp - 1).astype(jnp.int32)
    n_used = (pad_end[-1] // tm).astype(jnp.int32)

    ff = w_down.shape[2]
    tn_up = _largest_tile(ff, 512)
    tn_dn = _largest_tile(d, 2048)
    tile_start = (pad_start // tm).astype(jnp.int32)
    tile_count = (padded // tm).astype(jnp.int32)
    sched_up = _moe_schedule(tile_e, tile_start, tile_count, n_used, ff // tn_up, n_tiles)
    sched_dn = _moe_schedule(tile_e, tile_start, tile_count, n_used, d // tn_dn, n_tiles)

    x_sorted = _dispatch(dest, (pad_start + counts).astype(jnp.int32), pad_end.astype(jnp.int32),
                         n_used.reshape(1), u2, bsz, n, t_tok, n_tiles)
    act = _moe_up(sched_up, x_sorted, w_gate_up[0], b_gate_up[0].astype(F32).reshape(n_exp, 1, -1), tn_up)
    y_sorted = _moe_down(sched_dn, act, w_down[0], b_down[0].astype(F32).reshape(n_exp, 1, d), tn_dn)

    dest_seq = dest.reshape(bsz, t_tok, TOP_K)[:, NUM_META:, :].reshape(-1)
    gates_seq = gate_w.reshape(bsz, t_tok, TOP_K)[:, NUM_META:, :].reshape(bsz * seq, TOP_K)
    return _final(dest_seq, h1.reshape(bsz, n, d), gates_seq, norm_final.reshape(1, d), y_sorted, seq, tn_dn)
```

```python
import functools

import jax
import jax.numpy as jnp
from jax import lax
from jax.experimental import pallas as pl
from jax.experimental.pallas import tpu as pltpu

F32 = jnp.float32
BF16 = jnp.bfloat16
U32 = jnp.uint32

NUM_META = 16
MLSTM_DV = 512
MLSTM_DQK = 256
HGRN_DK = 128
GATE_SOFTCAP = 15.0
TOP_K = 4
SWIGLU_LIMIT = 7.0
SWIGLU_ALPHA = 1.702
NORM_EPS = 1e-5
HEAD_NORM_EPS = 1e-6

LANES = 128
VMEM_LIMIT_BYTES = 56 * 1024 * 1024

CHUNK = 256
FRONT_PAD = CHUNK - NUM_META
GATE_LANES = LANES

MOE_TILE = 512

def _largest_tile(total, target):
    best = LANES
    t = LANES
    while t <= min(total, target):
        if total % t == 0:
            best = t
        t += LANES
    assert total % best == 0, (total, target)
    return best


def _log_sigmoid(x):
    return jnp.minimum(x, 0.0) - jnp.log1p(jnp.exp(-jnp.abs(x)))


def _sigmoid(x):
    return 1.0 / (1.0 + jnp.exp(-x))


def _pack_bf16_pair(lo, hi):
    lo_bits = lax.bitcast_convert_type(lo.astype(BF16).astype(F32), U32) >> 16
    hi_bits = lax.bitcast_convert_type(hi.astype(BF16).astype(F32), U32) & jnp.uint32(0xFFFF0000)
    return lo_bits | hi_bits


def _unpack_bf16_pair(w):
    lo = lax.bitcast_convert_type(w << 16, F32)
    hi = lax.bitcast_convert_type(w & jnp.uint32(0xFFFF0000), F32)
    return lo, hi


def _dot(a, b):
    return jnp.dot(a, b, preferred_element_type=F32)


def _dot_nt(a, b):
    return lax.dot_general(a, b, (((1,), (1,)), ((), ())), preferred_element_type=F32)


def _dot_tn(a, b):
    return lax.dot_general(a, b, (((0,), (0,)), ((), ())), preferred_element_type=F32)


def _dot_exact(a, b):
    return jnp.dot(a, b, preferred_element_type=F32, precision=lax.Precision.HIGHEST)


def _params(*sem):
    return pltpu.CompilerParams(dimension_semantics=sem, vmem_limit_bytes=VMEM_LIMIT_BYTES)


def _inproj_kernel(h_ref, gain_ref, wg_ref, w_ref, z_ref, g_ref, u_scr):
    @pl.when(pl.program_id(1) == 0)
    def _():
        x = h_ref[...]
        ms = jnp.mean(x * x, axis=-1, keepdims=True)
        u = (x * lax.rsqrt(ms + NORM_EPS) * gain_ref[...]).astype(BF16)
        u_scr[...] = u
        g_ref[...] = _dot(u, wg_ref[...])

    z_ref[...] = _dot(u_scr[...], w_ref[...]).astype(z_ref.dtype)


def _in_proj(hp, gain, w_main, w_gates):
    m, d = hp.shape
    zw = w_main.shape[1]
    tm = _largest_tile(m, 512)
    tn = _largest_tile(zw, 1024)
    return pl.pallas_call(
        _inproj_kernel,
        grid=(m // tm, zw // tn),
        in_specs=[
            pl.BlockSpec((tm, d), lambda i, j: (i, 0)),
            pl.BlockSpec((1, d), lambda i, j: (0, 0)),
            pl.BlockSpec((d, GATE_LANES), lambda i, j: (0, 0)),
            pl.BlockSpec((d, tn), lambda i, j: (0, j)),
        ],
        out_specs=[
            pl.BlockSpec((tm, tn), lambda i, j: (i, j)),
            pl.BlockSpec((tm, GATE_LANES), lambda i, j: (i, 0)),
        ],
        out_shape=[
            jax.ShapeDtypeStruct((m, zw), BF16),
            jax.ShapeDtypeStruct((m, GATE_LANES), F32),
        ],
        scratch_shapes=[pltpu.VMEM((tm, d), BF16)],
        compiler_params=_params("parallel", "arbitrary"),
        name="in_proj",
    )(hp, gain, w_gates, w_main)


def _mlstm_kernel(q_ref, k_ref, v_ref, o_ref, gc_ref, gr_ref, bc_ref, br_ref, gain_ref, out_ref,
                  ct, nst, mst, *, heads, chunk):
    hd = pl.program_id(1)
    c = pl.program_id(2)
    L = chunk
    neg_inf = -jnp.inf

    @pl.when(c == 0)
    def _():
        ct[...] = jnp.zeros_like(ct)
        nst[...] = jnp.zeros_like(nst)
        mst[...] = jnp.zeros_like(mst)

    pos_col = lax.broadcasted_iota(jnp.int32, (L, 1), 0) + c * L
    pos_row = lax.broadcasted_iota(jnp.int32, (1, L), 1) + c * L
    pad_col = pos_col < FRONT_PAD
    pad_row = pos_row < FRONT_PAD

    gcol = gc_ref[...] + bc_ref[...]
    gcol = GATE_SOFTCAP * jnp.tanh(gcol / GATE_SOFTCAP)
    lane = lax.broadcasted_iota(jnp.int32, gcol.shape, 1)
    lf_col_all = jnp.where(pad_col, 0.0, _log_sigmoid(gcol))
    li_col = jnp.sum(jnp.where(lane == hd, gcol, 0.0), axis=1, keepdims=True)
    li_col = jnp.where(pad_col, neg_inf, li_col)

    grow = gr_ref[...] + br_ref[...]
    grow = GATE_SOFTCAP * jnp.tanh(grow / GATE_SOFTCAP)
    sub = lax.broadcasted_iota(jnp.int32, grow.shape, 0)
    lf_row_all = jnp.where(pad_row, 0.0, _log_sigmoid(grow))
    li_row = jnp.sum(jnp.where(sub == hd, grow, 0.0), axis=0, keepdims=True)
    li_row = jnp.where(pad_row, neg_inf, li_row)

    r_i = lax.broadcasted_iota(jnp.int32, (L, L), 0)
    c_i = lax.broadcasted_iota(jnp.int32, (L, L), 1)
    causal = r_i >= c_i
    lower = causal.astype(F32)
    upper = (r_i <= c_i).astype(F32)
    a_col_all = _dot_exact(lower, lf_col_all)
    a_col = jnp.sum(jnp.where(lane == heads + hd, a_col_all, 0.0), axis=1, keepdims=True)
    a_row_all = _dot_exact(lf_row_all, upper)
    a_row = jnp.sum(jnp.where(sub == heads + hd, a_row_all, 0.0), axis=0, keepdims=True)

    m_prev = mst[...]
    inter = a_col + m_prev
    d_log = jnp.where(causal, a_col - a_row + li_row, neg_inf)
    m_row = jnp.maximum(inter, jnp.max(d_log, axis=1, keepdims=True))

    q = q_ref[...]
    kf = k_ref[...].astype(F32) * (MLSTM_DQK ** -0.5)
    kb = kf.astype(BF16)
    v = v_ref[...]
    scores = _dot_nt(q, kb) * jnp.exp(d_log - m_row)
    inter_w = jnp.exp(inter - m_row)
    num = _dot(scores.astype(BF16), v) + inter_w * _dot_nt(q, ct[...].astype(BF16))
    den = jnp.sum(scores, axis=1, keepdims=True) + inter_w * jnp.sum(
        q.astype(F32) * nst[...], axis=1, keepdims=True)
    h_out = num / jnp.maximum(jnp.abs(den), jnp.exp(-m_row))

    y = h_out * lax.rsqrt(jnp.mean(h_out * h_out, axis=-1, keepdims=True) + HEAD_NORM_EPS)
    y = y * gain_ref[...] * _sigmoid(o_ref[...].astype(F32))
    out_ref[...] = y.astype(out_ref.dtype)

    a_end = a_col[L - 1:L, :]
    dec_col = a_end - a_col + li_col
    dec_row = a_end - a_row + li_row
    m_new = jnp.maximum(a_end + m_prev, jnp.max(dec_row, axis=1, keepdims=True))
    w_col = jnp.exp(dec_col - m_new)
    carry_w = jnp.exp(a_end + m_prev - m_new)
    kw = kf * w_col
    ct[...] = carry_w * ct[...] + _dot_tn(v, kw.astype(BF16))
    nst[...] = carry_w * nst[...] + jnp.sum(kw, axis=0, keepdims=True)
    mst[...] = m_new


def _mlstm(z3, gates_col, gates_row, bias_col, bias_row, gain, heads):
    b, n, _ = z3.shape
    L = CHUNK
    nc = n // L
    dq, dv = MLSTM_DQK, MLSTM_DV
    assert dv == 2 * dq
    kern = functools.partial(_mlstm_kernel, heads=heads, chunk=L)
    return pl.pallas_call(
        kern,
        grid=(b, heads, nc),
        in_specs=[
            pl.BlockSpec((None, L, dq), lambda i, h, c: (i, c, h)),
            pl.BlockSpec((None, L, dq), lambda i, h, c: (i, c, heads + h)),
            pl.BlockSpec((None, L, dv), lambda i, h, c: (i, c, heads + h)),
            pl.BlockSpec((None, L, dv), lambda i, h, c: (i, c, 2 * heads + h)),
            pl.BlockSpec((None, L, GATE_LANES), lambda i, h, c: (i, c, 0)),
            pl.BlockSpec((None, 2 * heads, L), lambda i, h, c: (i, 0, c)),
            pl.BlockSpec((1, GATE_LANES), lambda i, h, c: (0, 0)),
            pl.BlockSpec((2 * heads, 1), lambda i, h, c: (0, 0)),
            pl.BlockSpec((1, dv), lambda i, h, c: (0, h)),
        ],
        out_specs=pl.BlockSpec((None, L, dv), lambda i, h, c: (i, c, h)),
        out_shape=jax.ShapeDtypeStruct((b, n, heads * dv), BF16),
        scratch_shapes=[pltpu.VMEM((dv, dq), F32), pltpu.VMEM((1, dq), F32), pltpu.VMEM((1, 1), F32)],
        compiler_params=_params("parallel", "parallel", "arbitrary"),
        name="mlstm",
    )(z3, z3, z3, z3, gates_col, gates_row, bias_col, bias_row, gain)


def _block_ref_rows(g, hs, L):
    d = g.shape[1]
    if hs >= 4:
        blk = 2 * hs
        g3 = g.reshape(L // blk, blk, d)
        return jnp.broadcast_to(g3[:, hs - 1:hs, :], g3.shape).reshape(L, d)
    row = lax.broadcasted_iota(jnp.int32, (L, 1), 0)
    if hs == 2:
        r = row & 3
        up1 = pltpu.roll(g, L - 1, 0)
        dn1 = pltpu.roll(g, 1, 0)
        dn2 = pltpu.roll(g, 2, 0)
        return jnp.where(r == 0, up1, jnp.where(r == 1, g, jnp.where(r == 2, dn1, dn2)))
    assert hs == 1
    return jnp.where((row & 1) == 1, pltpu.roll(g, 1, 0), g)


def _hgrn_kernel(q_ref, f_ref, i_ref, g_ref, lb_ref, gain_ref, out_ref, st, *, chunk):
    c = pl.program_id(2)
    L = chunk

    @pl.when(c == 0)
    def _():
        st[...] = jnp.zeros_like(st)

    row = lax.broadcasted_iota(jnp.int32, (L, 1), 0)
    is_pad = (row + c * L) < FRONT_PAD

    lb = lb_ref[...]
    f_pre = f_ref[...].astype(F32)
    t0 = jnp.log(lb)
    t1 = jnp.log1p(-lb) + _log_sigmoid(f_pre)
    log_f = jnp.maximum(t0, t1) + jnp.log1p(jnp.exp(-jnp.abs(t0 - t1)))
    kh = (1.0 - lb) * _sigmoid(-f_pre)
    log_f = jnp.where(is_pad, 0.0, log_f)
    kh = jnp.where(is_pad, 0.0, kh)
    qf = q_ref[...].astype(F32)
    qh = qf * _sigmoid(qf) * (HGRN_DK ** -0.5)
    v = i_ref[...]

    r_i = lax.broadcasted_iota(jnp.int32, (L, L), 0)
    c_i = lax.broadcasted_iota(jnp.int32, (L, L), 1)
    lower = (r_i >= c_i).astype(F32)
    g_cum = _dot_exact(lower, log_f)

    o = _dot_nt((qh * jnp.exp(g_cum)).astype(BF16), st[...].astype(BF16))

    diag = jnp.sum(qh * kh, axis=1, keepdims=True)
    attn = jnp.where(r_i == c_i, diag, 0.0)
    hs = L // 2
    while hs >= 1:
        dj = g_cum - _block_ref_rows(g_cum, hs, L)
        e = jnp.exp(-jnp.abs(dj))
        second = (row & hs) != 0
        qj = jnp.where(second, qh * e, 0.0).astype(BF16)
        kj = jnp.where(second, 0.0, kh * e).astype(BF16)
        a = _dot_nt(qj, kj)
        if 2 * hs == L:
            attn = attn + a
        else:
            shift = (2 * hs).bit_length() - 1
            attn = attn + jnp.where((r_i >> shift) == (c_i >> shift), a, 0.0)
        hs //= 2
    o = o + _dot(attn.astype(BF16), v)

    y = o * lax.rsqrt(jnp.mean(o * o, axis=-1, keepdims=True) + HEAD_NORM_EPS)
    gate = g_ref[...].astype(F32)
    y = y * gain_ref[...] * (gate * _sigmoid(gate))
    out_ref[...] = y.astype(out_ref.dtype)

    g_end = g_cum[L - 1:L, :]
    k_dec = (kh * jnp.exp(g_end - g_cum)).astype(BF16)
    st[...] = jnp.exp(g_end) * st[...] + _dot_tn(v, k_dec)


def _hgrn(z3, lb, gain, col0, heads):
    b, n, _ = z3.shape
    L = CHUNK
    nc = n // L
    dk = HGRN_DK
    base = col0 // dk
    kern = functools.partial(_hgrn_kernel, chunk=L)

    def zspec(k):
        return pl.BlockSpec((None, L, dk), lambda i, h, c: (i, c, base + k * heads + h))

    return pl.pallas_call(
        kern,
        grid=(b, heads, nc),
        in_specs=[zspec(0), zspec(1), zspec(2), zspec(3),
                  pl.BlockSpec((1, dk), lambda i, h, c: (0, h)),
                  pl.BlockSpec((1, dk), lambda i, h, c: (0, h))],
        out_specs=pl.BlockSpec((None, L, dk), lambda i, h, c: (i, c, h)),
        out_shape=jax.ShapeDtypeStruct((b, n, heads * dk), BF16),
        scratch_shapes=[pltpu.VMEM((dk, dk), F32)],
        compiler_params=_params("parallel", "parallel", "arbitrary"),
        name="hgrn2",
    )(z3, z3, z3, z3, lb, gain)


def _outproj_kernel(hm_ref, oh_ref, w_ref, res_ref, gain_ref, wr_ref, br_ref,
                    h1_ref, u2_ref, lg_ref, *, nk):
    k = pl.program_id(1)

    @pl.when(k == 0)
    def _():
        h1_ref[...] = res_ref[...]

    @pl.when(k < nk)
    def _():
        h1_ref[...] += _dot(hm_ref[...], w_ref[...])

    @pl.when(k >= nk)
    def _():
        h1_ref[...] += _dot(oh_ref[...], w_ref[...])

    @pl.when(k == pl.num_programs(1) - 1)
    def _():
        h1 = h1_ref[...]
        u2 = h1 * lax.rsqrt(jnp.mean(h1 * h1, axis=-1, keepdims=True) + NORM_EPS) * gain_ref[...]
        half = u2.shape[1] // 2
        u2_ref[...] = _pack_bf16_pair(u2[:, :half], u2[:, half:])
        lg_ref[...] = _dot_exact(u2, wr_ref[...]) + br_ref[...]


def _out_proj(hm, oh, w_out_b, hp, gain, w_router, b_router):
    m, w1 = hm.shape
    w2 = oh.shape[1]
    d = w_out_b.shape[1]
    assert w1 == w2
    tm = _largest_tile(m, 384)
    tk = _largest_tile(w1, 512)
    nk = w1 // tk
    return pl.pallas_call(
        functools.partial(_outproj_kernel, nk=nk),
        grid=(m // tm, 2 * nk),
        in_specs=[
            pl.BlockSpec((tm, tk), lambda i, k: (i, jnp.minimum(k, nk - 1))),
            pl.BlockSpec((tm, tk), lambda i, k: (i, jnp.maximum(k - nk, 0))),
            pl.BlockSpec((tk, d), lambda i, k: (k, 0)),
            pl.BlockSpec((tm, d), lambda i, k: (i, 0)),
            pl.BlockSpec((1, d), lambda i, k: (0, 0)),
            pl.BlockSpec((d, LANES), lambda i, k: (0, 0)),
            pl.BlockSpec((1, LANES), lambda i, k: (0, 0)),
        ],
        out_specs=[
            pl.BlockSpec((tm, d), lambda i, k: (i, 0)),
            pl.BlockSpec((tm, d // 2), lambda i, k: (i, 0)),
            pl.BlockSpec((tm, LANES), lambda i, k: (i, 0)),
        ],
        out_shape=[
            jax.ShapeDtypeStruct((m, d), F32),
            jax.ShapeDtypeStruct((m, d // 2), U32),
            jax.ShapeDtypeStruct((m, LANES), F32),
        ],
        compiler_params=_params("parallel", "arbitrary"),
        name="out_proj",
    )(hm, oh, w_out_b, hp, gain, w_router, b_router)


def _moe_up_kernel(st_ref, se_ref, sj_ref, sf_ref, nu_ref, x_ref, wg_ref, wl_ref, bg_ref, bl_ref,
                   act_ref, wgb, wlb):
    s = pl.program_id(0)
    valid = s < nu_ref[0]

    @pl.when(jnp.logical_and(valid, sf_ref[s] == 1))
    def _():
        wgb[...] = wg_ref[...].astype(BF16)
        wlb[...] = wl_ref[...].astype(BF16)

    @pl.when(valid)
    def _():
        half = wgb.shape[0] // 2
        lo, hi = _unpack_bf16_pair(x_ref[...])
        lo = lo.astype(BF16)
        hi = hi.astype(BF16)
        gate = _dot(lo, wgb[:half, :]) + _dot(hi, wgb[half:, :]) + bg_ref[...]
        lin = _dot(lo, wlb[:half, :]) + _dot(hi, wlb[half:, :]) + bl_ref[...]
        gate = jnp.minimum(gate, SWIGLU_LIMIT)
        lin = jnp.clip(lin, -SWIGLU_LIMIT, SWIGLU_LIMIT)
        act_ref[...] = (gate * _sigmoid(SWIGLU_ALPHA * gate) * (lin + 1.0)).astype(act_ref.dtype)

    @pl.when(jnp.logical_not(valid))
    def _():
        act_ref[...] = jnp.zeros_like(act_ref)


def _moe_down_kernel(st_ref, se_ref, sj_ref, sf_ref, nu_ref, a_ref, wd_ref, bd_ref, y_ref, wdb):
    s = pl.program_id(0)
    valid = s < nu_ref[0]

    @pl.when(jnp.logical_and(valid, sf_ref[s] == 1))
    def _():
        wdb[...] = wd_ref[...].astype(BF16)

    @pl.when(valid)
    def _():
        y = _dot(a_ref[...], wdb[...]) + bd_ref[...]
        half = y.shape[1] // 2
        y_ref[...] = _pack_bf16_pair(y[:, :half], y[:, half:])

    @pl.when(jnp.logical_not(valid))
    def _():
        y_ref[...] = jnp.zeros_like(y_ref)


def _moe_schedule(tile_e, tile_start, tile_count, n_used, n_blocks, n_tiles):
    s = jnp.arange(n_blocks * n_tiles, dtype=jnp.int32)
    used = s < n_used * n_blocks
    e = jnp.repeat(tile_e, n_blocks)
    onehot = e[:, None] == jnp.arange(tile_start.shape[0], dtype=jnp.int32)[None, :]
    start_e = jnp.sum(jnp.where(onehot, tile_start[None, :], 0), axis=1)
    nt = jnp.maximum(jnp.sum(jnp.where(onehot, tile_count[None, :], 0), axis=1), 1)
    local = s - n_blocks * start_e
    sj = jnp.where(used, local // nt, s % n_blocks).astype(jnp.int32)
    st = jnp.where(used, start_e + local % nt, s // n_blocks).astype(jnp.int32)
    sf = jnp.logical_and(used, local % nt == 0).astype(jnp.int32)
    nu = (n_used * n_blocks).astype(jnp.int32).reshape(1)
    return st, e.astype(jnp.int32), sj, sf, nu


def _clamped_step(s, nu):
    return jnp.minimum(s, nu[0] - 1)


def _moe_up(sched, x_sorted, w_gu, b_gu, tn_up):
    p, half = x_sorted.shape
    d = 2 * half
    ff = w_gu.shape[2] // 2
    tm = MOE_TILE
    n_up = ff // tn_up
    n_steps = sched[0].shape[0]
    cl = _clamped_step

    grid_spec = pltpu.PrefetchScalarGridSpec(
        num_scalar_prefetch=5,
        grid=(n_steps,),
        in_specs=[
            pl.BlockSpec((tm, half), lambda s, st, se, sj, sf, nu: (st[cl(s, nu)], 0)),
            pl.BlockSpec((None, d, tn_up), lambda s, st, se, sj, sf, nu: (se[cl(s, nu)], 0, sj[cl(s, nu)])),
            pl.BlockSpec((None, d, tn_up),
                         lambda s, st, se, sj, sf, nu: (se[cl(s, nu)], 0, n_up + sj[cl(s, nu)])),
            pl.BlockSpec((None, 1, tn_up), lambda s, st, se, sj, sf, nu: (se[cl(s, nu)], 0, sj[cl(s, nu)])),
            pl.BlockSpec((None, 1, tn_up),
                         lambda s, st, se, sj, sf, nu: (se[cl(s, nu)], 0, n_up + sj[cl(s, nu)])),
        ],
        out_specs=pl.BlockSpec((tm, tn_up), lambda s, st, se, sj, sf, nu: (st[s], sj[s])),
        scratch_shapes=[pltpu.VMEM((d, tn_up), BF16), pltpu.VMEM((d, tn_up), BF16)],
    )
    return pl.pallas_call(
        _moe_up_kernel,
        grid_spec=grid_spec,
        out_shape=jax.ShapeDtypeStruct((p, ff), BF16),
        compiler_params=_params("arbitrary"),
        name="moe_up",
    )(*sched, x_sorted, w_gu, w_gu, b_gu, b_gu)


def _moe_down(sched, act, w_dn, b_dn, tn_dn):
    p, ff = act.shape
    d = w_dn.shape[2]
    tm = MOE_TILE
    n_steps = sched[0].shape[0]
    cl = _clamped_step

    grid_spec = pltpu.PrefetchScalarGridSpec(
        num_scalar_prefetch=5,
        grid=(n_steps,),
        in_specs=[
            pl.BlockSpec((tm, ff), lambda s, st, se, sj, sf, nu: (st[cl(s, nu)], 0)),
            pl.BlockSpec((None, ff, tn_dn), lambda s, st, se, sj, sf, nu: (se[cl(s, nu)], 0, sj[cl(s, nu)])),
            pl.BlockSpec((None, 1, tn_dn), lambda s, st, se, sj, sf, nu: (se[cl(s, nu)], 0, sj[cl(s, nu)])),
        ],
        out_specs=pl.BlockSpec((tm, tn_dn // 2), lambda s, st, se, sj, sf, nu: (st[s], sj[s])),
        scratch_shapes=[pltpu.VMEM((ff, tn_dn), BF16)],
    )
    return pl.pallas_call(
        _moe_down_kernel,
        grid_spec=grid_spec,
        out_shape=jax.ShapeDtypeStruct((p, d // 2), U32),
        compiler_params=_params("arbitrary"),
        name="moe_down",
    )(*sched, act, w_dn, b_dn)


def _row_copy(src, src_row, dst, dst_row, sem, rows=1):
    return pltpu.make_async_copy(src.at[pl.ds(src_row, rows)], dst.at[pl.ds(dst_row, rows)], sem)


def _dispatch_kernel(dest_ref, zlo_ref, zhi_ref, nused_ref, x_ref, xs_hbm, zbuf, sem_tok, sem_zero,
                     *, nblk, t_tok, n_exp, n_tiles, tm):
    i = pl.program_id(0)
    b = i // nblk
    j = i % nblk
    rows = x_ref.shape[0]

    def zero_rows(wait):
        def per_expert(e, c):
            def body(p, c2):
                cp = _row_copy(zbuf, 0, xs_hbm, p, sem_zero)
                cp.wait() if wait else cp.start()
                return c2
            return lax.fori_loop(zlo_ref[e], zhi_ref[e], body, c)
        lax.fori_loop(0, n_exp, per_expert, 0)

    def zero_tiles(wait):
        def body(t, c):
            cp = _row_copy(zbuf, 0, xs_hbm, pl.multiple_of(t * tm, tm), sem_zero, rows=tm)
            cp.wait() if wait else cp.start()
            return c
        lax.fori_loop(nused_ref[0], n_tiles, body, 0)

    @pl.when(i == 0)
    def _():
        zbuf[...] = jnp.zeros_like(zbuf)
        zero_rows(False)
        zero_tiles(False)

    first = jnp.where(j == 0, FRONT_PAD, 0)
    tok0 = b * t_tok + j * rows - FRONT_PAD

    def tok_body(r, c):
        a0 = (tok0 + r) * TOP_K
        for k in range(TOP_K):
            _row_copy(x_ref, r, xs_hbm, dest_ref[a0 + k], sem_tok).start()
        return c
    lax.fori_loop(first, rows, tok_body, 0)

    @pl.when(j == 0)
    def _():
        def wait_body(r, c):
            for k in range(TOP_K):
                _row_copy(x_ref, 0, xs_hbm, 0, sem_tok).wait()
            return c
        lax.fori_loop(FRONT_PAD, rows, wait_body, 0)

    @pl.when(j != 0)
    def _():
        for k in range(TOP_K):
            _row_copy(x_ref, 0, xs_hbm, 0, sem_tok, rows=rows).wait()

    @pl.when(i == 0)
    def _():
        zero_rows(True)
        zero_tiles(True)


def _dispatch(dest, zlo, zhi, n_used, u2p, bsz, n, t_tok, n_tiles):
    half = u2p.shape[1]
    tm = MOE_TILE
    rows = CHUNK
    nblk = n // rows
    kern = functools.partial(_dispatch_kernel, nblk=nblk, t_tok=t_tok, n_exp=zlo.shape[0],
                             n_tiles=n_tiles, tm=tm)
    grid_spec = pltpu.PrefetchScalarGridSpec(
        num_scalar_prefetch=4,
        grid=(bsz * nblk,),
        in_specs=[pl.BlockSpec((rows, half), lambda i, de, lo, hi, nu: (i, 0))],
        out_specs=pl.BlockSpec(memory_space=pl.ANY),
        scratch_shapes=[pltpu.VMEM((tm, half), U32), pltpu.SemaphoreType.DMA(()), pltpu.SemaphoreType.DMA(())],
    )
    return pl.pallas_call(
        kern,
        grid_spec=grid_spec,
        out_shape=jax.ShapeDtypeStruct((n_tiles * tm, half), U32),
        compiler_params=_params("arbitrary"),
        name="moe_dispatch",
    )(dest, zlo, zhi, n_used, u2p)


def _final_kernel(dest_ref, h1_ref, g_ref, gain_ref, y_hbm, out_ref, ybuf, sem, *, tm, n_tiles, tn_dn):
    i = pl.program_id(0)

    def gather(tile, slot, wait):
        base = tile * (tm * TOP_K)
        if wait:
            for k in range(TOP_K):
                pltpu.make_async_copy(y_hbm.at[pl.ds(0, tm)], ybuf.at[slot, k], sem.at[slot]).wait()
            return

        def body(r, c):
            for k in range(TOP_K):
                pltpu.make_async_copy(y_hbm.at[pl.ds(dest_ref[base + r * TOP_K + k], 1)],
                                      ybuf.at[slot, k, pl.ds(r, 1)], sem.at[slot]).start()
            return c
        lax.fori_loop(0, tm, body, 0)

    @pl.when(i == 0)
    def _():
        gather(0, 0, False)

    @pl.when(i + 1 < n_tiles)
    def _():
        gather(i + 1, (i + 1) % 2, False)

    slot = i % 2
    gather(i, slot, True)

    h = h1_ref[...]
    g = g_ref[...]
    hw = tn_dn // 2
    for k in range(TOP_K):
        w = ybuf[slot, k]
        parts = []
        for jb in range(w.shape[1] // hw):
            lo, hi = _unpack_bf16_pair(w[:, jb * hw:(jb + 1) * hw])
            parts += [lo, hi]
        h = h + g[:, k:k + 1] * jnp.concatenate(parts, axis=1)
    out_ref[...] = h * lax.rsqrt(jnp.mean(h * h, axis=-1, keepdims=True) + NORM_EPS) * gain_ref[...]


def _final(dest_seq, h1_3, gates_seq, gain, y_sorted, seq, tn_dn):
    b, n, d = h1_3.shape
    tm = CHUNK
    skip = (n - seq) // tm
    nblk = seq // tm
    n_tiles = b * nblk
    kern = functools.partial(_final_kernel, tm=tm, n_tiles=n_tiles, tn_dn=tn_dn)
    grid_spec = pltpu.PrefetchScalarGridSpec(
        num_scalar_prefetch=1,
        grid=(n_tiles,),
        in_specs=[
            pl.BlockSpec((None, tm, d), lambda i, ds: (i // nblk, skip + i % nblk, 0)),
            pl.BlockSpec((tm, TOP_K), lambda i, ds: (i, 0)),
            pl.BlockSpec((1, d), lambda i, ds: (0, 0)),
            pl.BlockSpec(memory_space=pl.ANY),
        ],
        out_specs=pl.BlockSpec((None, tm, d), lambda i, ds: (i // nblk, i % nblk, 0)),
        scratch_shapes=[pltpu.VMEM((2, TOP_K, tm, d // 2), U32), pltpu.SemaphoreType.DMA((2,))],
    )
    return pl.pallas_call(
        kern,
        grid_spec=grid_spec,
        out_shape=jax.ShapeDtypeStruct((b, seq, d), F32),
        compiler_params=_params("arbitrary"),
        name="final_norm",
    )(dest_seq, h1_3, gates_seq, gain, y_sorted)


def kernel(x, meta_tokens, norm_mix, w_in, b_mlstm_gates, mlstm_head_norm, hgrn_lower_bound,
           hgrn_head_norm, w_out, norm_ffn, w_router, b_router, w_gate_up, b_gate_up, w_down,
           b_down, norm_final):
    bsz, seq, d = x.shape
    depth = w_in.shape[0]
    assert depth == 1 and seq % CHUNK == 0
    mh = b_mlstm_gates.shape[-1] // 2
    mw = mh * MLSTM_DV
    hw = hgrn_lower_bound.shape[-1]
    hh = hw // HGRN_DK
    n_exp = w_router.shape[-1]
    n = CHUNK + seq
    t_tok = NUM_META + seq

    meta = jnp.broadcast_to(meta_tokens[None].astype(x.dtype), (bsz, NUM_META, d))
    hp = jnp.concatenate([jnp.zeros((bsz, FRONT_PAD, d), x.dtype), meta, x], axis=1)
    hp2 = hp.reshape(bsz * n, d)

    g0 = 2 * mh * MLSTM_DQK + 2 * mw
    w_in0 = w_in[0]
    w_main = jnp.concatenate([w_in0[:, :g0], w_in0[:, g0 + 2 * mh:]], axis=1).astype(BF16)
    w_gates = jnp.pad(w_in0[:, g0:g0 + 2 * mh], ((0, 0), (0, GATE_LANES - 2 * mh))).astype(BF16)
    bias_col = jnp.pad(b_mlstm_gates[0].astype(F32), (0, GATE_LANES - 2 * mh)).reshape(1, GATE_LANES)
    bias_row = b_mlstm_gates[0].astype(F32).reshape(2 * mh, 1)
    lbs = jnp.cumsum(jax.nn.softmax(hgrn_lower_bound.astype(F32), axis=0), axis=0)[0].reshape(1, hw)

    z, gates = _in_proj(hp2, norm_mix[0].reshape(1, d), w_main, w_gates)
    z3 = z.reshape(bsz, n, -1)
    gates_col = gates.reshape(bsz, n, GATE_LANES)
    gates_row = jnp.swapaxes(gates_col[:, :, :2 * mh], 1, 2)
    hm = _mlstm(z3, gates_col, gates_row, bias_col, bias_row, mlstm_head_norm[0].reshape(1, mw), mh)
    oh = _hgrn(z3, lbs, hgrn_head_norm[0].reshape(1, hw), g0, hh)

    wr = jnp.pad(w_router[0].astype(F32), ((0, 0), (0, LANES - n_exp)))
    br = jnp.pad(b_router[0].astype(F32), (0, LANES - n_exp)).reshape(1, LANES)
    h1, u2, logits = _out_proj(hm.reshape(bsz * n, mw), oh.reshape(bsz * n, hw), w_out[0].astype(BF16),
                               hp2, norm_ffn[0].reshape(1, d), wr, br)

    t_all = bsz * t_tok
    tk = t_all * TOP_K
    logits_tok = logits.reshape(bsz, n, LANES)[:, FRONT_PAD:, :n_exp].reshape(t_all, n_exp)
    top_val, top_idx = lax.top_k(logits_tok, TOP_K)
    gate_w = jax.nn.softmax(top_val, axis=-1)
    flat_e = top_idx.reshape(-1).astype(jnp.int32)
    onehot = (flat_e[:, None] == jnp.arange(n_exp, dtype=jnp.int32)[None, :]).astype(jnp.int32)
    csum = jnp.cumsum(onehot, axis=0)
    rank = jnp.sum((csum - onehot) * onehot, axis=1)
    counts = csum[-1]
    tm = MOE_TILE
    padded = (counts + tm - 1) // tm * tm
    pad_end = jnp.cumsum(padded)
    pad_start = pad_end - padded
    dest = (jnp.sum(onehot * pad_start[None, :], axis=1) + rank).astype(jnp.int32)
    n_tiles = -(-(tk + n_exp * (tm - 1)) // tm)
    tile_row0 = jnp.arange(n_tiles, dtype=jnp.int32) * tm
    tile_e = jnp.minimum(jnp.sum((tile_row0[:, None] >= pad_end[None, :]).astype(jnp.int32), axis=1),
                         n_exp - 1).astype(jnp.int32)
    n_used = (pad_end[-1] // tm).astype(jnp.int32)

    ff = w_down.shape[2]
    tn_up = _largest_tile(ff, 512)
    tn_dn = _largest_tile(d, 2048)
    tile_start = (pad_start // tm).astype(jnp.int32)
    tile_count = (padded // tm).astype(jnp.int32)
    sched_up = _moe_schedule(tile_e, tile_start, tile_count, n_used, ff // tn_up, n_tiles)
    sched_dn = _moe_schedule(tile_e, tile_start, tile_count, n_used, d // tn_dn, n_tiles)

    x_sorted = _dispatch(dest, (pad_start + counts).astype(jnp.int32), pad_end.astype(jnp.int32),
                         n_used.reshape(1), u2, bsz, n, t_tok, n_tiles)
    act = _moe_up(sched_up, x_sorted, w_gate_up[0], b_gate_up[0].astype(F32).reshape(n_exp, 1, -1), tn_up)
    y_sorted = _moe_down(sched_dn, act, w_down[0], b_down[0].astype(F32).reshape(n_exp, 1, d), tn_dn)

    dest_seq = dest.reshape(bsz, t_tok, TOP_K)[:, NUM_META:, :].reshape(-1)
    gates_seq = gate_w.reshape(bsz, t_tok, TOP_K)[:, NUM_META:, :].reshape(bsz * seq, TOP_K)
    return _final(dest_seq, h1.reshape(bsz, n, d), gates_seq, norm_final.reshape(1, d), y_sorted, seq, tn_dn)
```

```python
import functools

import jax
import jax.numpy as jnp
from jax import lax
from jax.experimental import pallas as pl
from jax.experimental.pallas import tpu as pltpu

F32 = jnp.float32
BF16 = jnp.bfloat16
U32 = jnp.uint32

NUM_META = 16
MLSTM_DV = 512
MLSTM_DQK = 256
HGRN_DK = 128
GATE_SOFTCAP = 15.0
TOP_K = 4
SWIGLU_LIMIT = 7.0
SWIGLU_ALPHA = 1.702
NORM_EPS = 1e-5
HEAD_NORM_EPS = 1e-6
LOG2_E = 1.4426950408889634

LANES = 128
VMEM_LIMIT_BYTES = 58 * 1024 * 1024

CHUNK = 256
FRONT_PAD = CHUNK - NUM_META
GATE_LANES = LANES

MOE_TILE = 512

def _largest_tile(total, target):
    best = LANES
    t = LANES
    while t <= min(total, target):
        if total % t == 0:
            best = t
        t += LANES
    assert total % best == 0, (total, target)
    return best


def _log_sigmoid(x):
    return jnp.minimum(x, 0.0) - jnp.log1p(jnp.exp(-jnp.abs(x)))


def _sigmoid(x):
    return 1.0 / (1.0 + jnp.exp(-x))


def _pack_bf16_pair(lo, hi):
    lo_bits = lax.bitcast_convert_type(lo.astype(BF16).astype(F32), U32) >> 16
    hi_bits = lax.bitcast_convert_type(hi.astype(BF16).astype(F32), U32) & jnp.uint32(0xFFFF0000)
    return lo_bits | hi_bits


def _unpack_bf16_pair(w):
    lo = lax.bitcast_convert_type(w << 16, F32)
    hi = lax.bitcast_convert_type(w & jnp.uint32(0xFFFF0000), F32)
    return lo, hi


def _dot(a, b):
    return jnp.dot(a, b, preferred_element_type=F32)


def _dot_nt(a, b):
    return lax.dot_general(a, b, (((1,), (1,)), ((), ())), preferred_element_type=F32)


def _dot_tn(a, b):
    return lax.dot_general(a, b, (((0,), (0,)), ((), ())), preferred_element_type=F32)


def _dot_exact(a, b):
    return jnp.dot(a, b, preferred_element_type=F32, precision=lax.Precision.HIGHEST)


def _params(*sem):
    return pltpu.CompilerParams(dimension_semantics=sem, vmem_limit_bytes=VMEM_LIMIT_BYTES)


def _inproj_kernel(h_ref, gain_ref, wg_ref, w_ref, z_ref, g_ref, u_scr):
    @pl.when(pl.program_id(1) == 0)
    def _():
        x = h_ref[...]
        ms = jnp.mean(x * x, axis=-1, keepdims=True)
        u = (x * lax.rsqrt(ms + NORM_EPS) * gain_ref[...]).astype(BF16)
        u_scr[...] = u
        g_ref[...] = _dot(u, wg_ref[...])

    z_ref[...] = _dot(u_scr[...], w_ref[...]).astype(z_ref.dtype)


def _in_proj(hp, gain, w_main, w_gates):
    m, d = hp.shape
    zw = w_main.shape[1]
    tm = _largest_tile(m, 512)
    tn = _largest_tile(zw, 1024)
    return pl.pallas_call(
        _inproj_kernel,
        grid=(m // tm, zw // tn),
        in_specs=[
            pl.BlockSpec((tm, d), lambda i, j: (i, 0)),
            pl.BlockSpec((1, d), lambda i, j: (0, 0)),
            pl.BlockSpec((d, GATE_LANES), lambda i, j: (0, 0)),
            pl.BlockSpec((d, tn), lambda i, j: (0, j)),
        ],
        out_specs=[
            pl.BlockSpec((tm, tn), lambda i, j: (i, j)),
            pl.BlockSpec((tm, GATE_LANES), lambda i, j: (i, 0)),
        ],
        out_shape=[
            jax.ShapeDtypeStruct((m, zw), BF16),
            jax.ShapeDtypeStruct((m, GATE_LANES), F32),
        ],
        scratch_shapes=[pltpu.VMEM((tm, d), BF16)],
        compiler_params=_params("parallel", "arbitrary"),
        name="in_proj",
    )(hp, gain, w_gates, w_main)


def _mlstm_kernel(q_ref, k_ref, v_ref, o_ref, gc_ref, gr_ref, bc_ref, br_ref, gain_ref, out_ref,
                  ct, nst, mst, *, heads, chunk):
    hd = pl.program_id(1)
    c = pl.program_id(2)
    L = chunk
    neg_inf = -jnp.inf

    @pl.when(c == 0)
    def _():
        ct[...] = jnp.zeros_like(ct)
        nst[...] = jnp.zeros_like(nst)
        mst[...] = jnp.zeros_like(mst)

    pos_col = lax.broadcasted_iota(jnp.int32, (L, 1), 0) + c * L
    pos_row = lax.broadcasted_iota(jnp.int32, (1, L), 1) + c * L
    pad_col = pos_col < FRONT_PAD
    pad_row = pos_row < FRONT_PAD

    gcol = gc_ref[...] + bc_ref[...]
    gcol = GATE_SOFTCAP * jnp.tanh(gcol / GATE_SOFTCAP)
    lane = lax.broadcasted_iota(jnp.int32, gcol.shape, 1)
    lf_col_all = jnp.where(pad_col, 0.0, _log_sigmoid(gcol))
    li_col = jnp.sum(jnp.where(lane == hd, gcol, 0.0), axis=1, keepdims=True)
    li_col = jnp.where(pad_col, neg_inf, li_col)

    grow = gr_ref[...] + br_ref[...]
    grow = GATE_SOFTCAP * jnp.tanh(grow / GATE_SOFTCAP)
    sub = lax.broadcasted_iota(jnp.int32, grow.shape, 0)
    lf_row_all = jnp.where(pad_row, 0.0, _log_sigmoid(grow))
    li_row = jnp.sum(jnp.where(sub == hd, grow, 0.0), axis=0, keepdims=True)
    li_row = jnp.where(pad_row, neg_inf, li_row)

    r_i = lax.broadcasted_iota(jnp.int32, (L, L), 0)
    c_i = lax.broadcasted_iota(jnp.int32, (L, L), 1)
    causal = r_i >= c_i
    lower = causal.astype(F32)
    upper = (r_i <= c_i).astype(F32)
    a_col_all = _dot_exact(lower, lf_col_all)
    a_col = jnp.sum(jnp.where(lane == heads + hd, a_col_all, 0.0), axis=1, keepdims=True)
    a_row_all = _dot_exact(lf_row_all, upper)
    a_row = jnp.sum(jnp.where(sub == heads + hd, a_row_all, 0.0), axis=0, keepdims=True)

    m_prev = mst[...]
    inter = a_col + m_prev
    d_log = jnp.where(causal, a_col - a_row + li_row, neg_inf)
    m_row = jnp.maximum(inter, jnp.max(d_log, axis=1, keepdims=True))

    q = q_ref[...]
    kf = k_ref[...].astype(F32) * (MLSTM_DQK ** -0.5)
    kb = kf.astype(BF16)
    v = v_ref[...]
    scores = _dot_nt(q, kb) * jnp.exp(d_log - m_row)
    inter_w = jnp.exp(inter - m_row)
    num = _dot(scores.astype(BF16), v) + inter_w * _dot_nt(q, ct[...].astype(BF16))
    den = jnp.sum(scores, axis=1, keepdims=True) + inter_w * jnp.sum(
        q.astype(F32) * nst[...], axis=1, keepdims=True)
    h_out = num / jnp.maximum(jnp.abs(den), jnp.exp(-m_row))

    y = h_out * lax.rsqrt(jnp.mean(h_out * h_out, axis=-1, keepdims=True) + HEAD_NORM_EPS)
    y = y * gain_ref[...] * _sigmoid(o_ref[...].astype(F32))
    out_ref[...] = y.astype(out_ref.dtype)

    a_end = a_col[L - 1:L, :]
    dec_col = a_end - a_col + li_col
    dec_row = a_end - a_row + li_row
    m_new = jnp.maximum(a_end + m_prev, jnp.max(dec_row, axis=1, keepdims=True))
    w_col = jnp.exp(dec_col - m_new)
    carry_w = jnp.exp(a_end + m_prev - m_new)
    kw = kf * w_col
    ct[...] = carry_w * ct[...] + _dot_tn(v, kw.astype(BF16))
    nst[...] = carry_w * nst[...] + jnp.sum(kw, axis=0, keepdims=True)
    mst[...] = m_new


def _mlstm(z3, gates_col, gates_row, bias_col, bias_row, gain, heads):
    b, n, _ = z3.shape
    L = CHUNK
    nc = n // L
    dq, dv = MLSTM_DQK, MLSTM_DV
    assert dv == 2 * dq
    kern = functools.partial(_mlstm_kernel, heads=heads, chunk=L)
    return pl.pallas_call(
        kern,
        grid=(b, heads, nc),
        in_specs=[
            pl.BlockSpec((None, L, dq), lambda i, h, c: (i, c, h)),
            pl.BlockSpec((None, L, dq), lambda i, h, c: (i, c, heads + h)),
            pl.BlockSpec((None, L, dv), lambda i, h, c: (i, c, heads + h)),
            pl.BlockSpec((None, L, dv), lambda i, h, c: (i, c, 2 * heads + h)),
            pl.BlockSpec((None, L, GATE_LANES), lambda i, h, c: (i, c, 0)),
            pl.BlockSpec((None, 2 * heads, L), lambda i, h, c: (i, 0, c)),
            pl.BlockSpec((1, GATE_LANES), lambda i, h, c: (0, 0)),
            pl.BlockSpec((2 * heads, 1), lambda i, h, c: (0, 0)),
            pl.BlockSpec((1, dv), lambda i, h, c: (0, h)),
        ],
        out_specs=pl.BlockSpec((None, L, dv), lambda i, h, c: (i, c, h)),
        out_shape=jax.ShapeDtypeStruct((b, n, heads * dv), BF16),
        scratch_shapes=[pltpu.VMEM((dv, dq), F32), pltpu.VMEM((1, dq), F32), pltpu.VMEM((1, 1), F32)],
        compiler_params=_params("parallel", "parallel", "arbitrary"),
        name="mlstm",
    )(z3, z3, z3, z3, gates_col, gates_row, bias_col, bias_row, gain)


def _block_ref_rows(g, hs, L):
    d = g.shape[1]
    if hs >= 4:
        blk = 2 * hs
        g3 = g.reshape(L // blk, blk, d)
        return jnp.broadcast_to(g3[:, hs - 1:hs, :], g3.shape).reshape(L, d)
    row = lax.broadcasted_iota(jnp.int32, (L, 1), 0)
    if hs == 2:
        r = row & 3
        up1 = pltpu.roll(g, L - 1, 0)
        dn1 = pltpu.roll(g, 1, 0)
        dn2 = pltpu.roll(g, 2, 0)
        return jnp.where(r == 0, up1, jnp.where(r == 1, g, jnp.where(r == 2, dn1, dn2)))
    assert hs == 1
    return jnp.where((row & 1) == 1, pltpu.roll(g, 1, 0), g)


def _hgrn_kernel(q_ref, f_ref, i_ref, g_ref, lb_ref, gain_ref, tri_ref, lvl_ref, out_ref, st,
                 *, chunk, heads_per_step):
    c = pl.program_id(2)
    L = chunk
    dk = HGRN_DK

    @pl.when(c == 0)
    def _():
        st[...] = jnp.zeros_like(st)

    row = lax.broadcasted_iota(jnp.int32, (L, 1), 0)
    is_pad = (row + c * L) < FRONT_PAD
    tri = tri_ref[...]

    for h in range(heads_per_step):
        cols = slice(h * dk, (h + 1) * dk)
        lb = lb_ref[:, cols]
        f_pre = f_ref[:, cols].astype(F32)
        t0 = jnp.log(lb)
        t1 = jnp.log1p(-lb) + _log_sigmoid(f_pre)
        log_f = jnp.maximum(t0, t1) + jnp.log1p(jnp.exp(-jnp.abs(t0 - t1)))
        kh = (1.0 - lb) * _sigmoid(-f_pre)
        log_f = jnp.where(is_pad, 0.0, log_f)
        kh = jnp.where(is_pad, 0.0, kh)
        qf = q_ref[:, cols].astype(F32)
        qh = qf * _sigmoid(qf) * (HGRN_DK ** -0.5)
        v = i_ref[:, cols]

        f_hi = log_f.astype(BF16)
        f_lo = (log_f - f_hi.astype(F32)).astype(BF16)
        g2 = _dot(tri, jnp.concatenate([f_hi, f_lo], axis=1))
        g_cum = g2[:, :dk] + g2[:, dk:]
        g_log2 = g_cum * LOG2_E

        o = _dot_nt((qh * jnp.exp2(g_log2)).astype(BF16), st[h].astype(BF16))

        diag = jnp.sum(qh * kh, axis=1, keepdims=True)
        attn = jnp.where(lvl_ref[...] == 0, diag, 0.0)
        hs = 1
        while hs < L:
            dj = g_log2 - _block_ref_rows(g_log2, hs, L)
            second = (row & hs) != 0
            qj = (qh * jnp.exp2(jnp.where(second, dj, -jnp.inf))).astype(BF16)
            kj = (kh * jnp.exp2(jnp.where(second, -jnp.inf, -dj))).astype(BF16)
            attn = jnp.where(lvl_ref[...] >= hs, _dot_nt(qj, kj), attn)
            hs *= 2
        o = o + _dot(attn.astype(BF16), v)

        y = o * lax.rsqrt(jnp.mean(o * o, axis=-1, keepdims=True) + HEAD_NORM_EPS)
        gate = g_ref[:, cols].astype(F32)
        y = y * gain_ref[:, cols] * (gate * _sigmoid(gate))
        out_ref[:, cols] = y.astype(out_ref.dtype)

        g_end = g_log2[L - 1:L, :]
        k_dec = (kh * jnp.exp2(g_end - g_log2)).astype(BF16)
        st[h] = jnp.exp2(g_end) * st[h] + _dot_tn(v, k_dec)


def _hgrn(z3, lb, gain, col0, heads):
    b, n, _ = z3.shape
    L = CHUNK
    nc = n // L
    dk = HGRN_DK
    hb = 2 if (heads % 2 == 0 and (col0 // dk) % 2 == 0) else 1
    wb = hb * dk
    base = col0 // wb
    kern = functools.partial(_hgrn_kernel, chunk=L, heads_per_step=hb)
    r_i = lax.broadcasted_iota(jnp.int32, (L, L), 0)
    c_i = lax.broadcasted_iota(jnp.int32, (L, L), 1)
    tri = (r_i >= c_i).astype(BF16)
    lvl = r_i ^ c_i

    def zspec(k):
        return pl.BlockSpec((None, L, wb), lambda i, h, c: (i, c, base + k * (heads // hb) + h))

    return pl.pallas_call(
        kern,
        grid=(b, heads // hb, nc),
        in_specs=[zspec(0), zspec(1), zspec(2), zspec(3),
                  pl.BlockSpec((1, wb), lambda i, h, c: (0, h)),
                  pl.BlockSpec((1, wb), lambda i, h, c: (0, h)),
                  pl.BlockSpec((L, L), lambda i, h, c: (0, 0)),
                  pl.BlockSpec((L, L), lambda i, h, c: (0, 0))],
        out_specs=pl.BlockSpec((None, L, wb), lambda i, h, c: (i, c, h)),
        out_shape=jax.ShapeDtypeStruct((b, n, heads * dk), BF16),
        scratch_shapes=[pltpu.VMEM((hb, dk, dk), F32)],
        compiler_params=_params("parallel", "parallel", "arbitrary"),
        name="hgrn2",
    )(z3, z3, z3, z3, lb, gain, tri, lvl)


def _outproj_kernel(hm_ref, oh_ref, w_ref, res_ref, gain_ref, wr_ref, br_ref,
                    h1_ref, u2_ref, lg_ref, *, nk):
    k = pl.program_id(1)

    @pl.when(k == 0)
    def _():
        h1_ref[...] = res_ref[...]

    @pl.when(k < nk)
    def _():
        h1_ref[...] += _dot(hm_ref[...], w_ref[...])

    @pl.when(k >= nk)
    def _():
        h1_ref[...] += _dot(oh_ref[...], w_ref[...])

    @pl.when(k == pl.num_programs(1) - 1)
    def _():
        h1 = h1_ref[...]
        u2 = h1 * lax.rsqrt(jnp.mean(h1 * h1, axis=-1, keepdims=True) + NORM_EPS) * gain_ref[...]
        half = u2.shape[1] // 2
        u2_ref[...] = _pack_bf16_pair(u2[:, :half], u2[:, half:])
        lg_ref[...] = _dot_exact(u2, wr_ref[...]) + br_ref[...]


def _out_proj(hm, oh, w_out_b, hp, gain, w_router, b_router):
    m, w1 = hm.shape
    w2 = oh.shape[1]
    d = w_out_b.shape[1]
    assert w1 == w2
    tm = _largest_tile(m, 384)
    tk = _largest_tile(w1, 1024)
    nk = w1 // tk
    return pl.pallas_call(
        functools.partial(_outproj_kernel, nk=nk),
        grid=(m // tm, 2 * nk),
        in_specs=[
            pl.BlockSpec((tm, tk), lambda i, k: (i, jnp.minimum(k, nk - 1))),
            pl.BlockSpec((tm, tk), lambda i, k: (i, jnp.maximum(k - nk, 0))),
            pl.BlockSpec((tk, d), lambda i, k: (k, 0)),
            pl.BlockSpec((tm, d), lambda i, k: (i, 0)),
            pl.BlockSpec((1, d), lambda i, k: (0, 0)),
            pl.BlockSpec((d, LANES), lambda i, k: (0, 0)),
            pl.BlockSpec((1, LANES), lambda i, k: (0, 0)),
        ],
        out_specs=[
            pl.BlockSpec((tm, d), lambda i, k: (i, 0)),
            pl.BlockSpec((tm, d // 2), lambda i, k: (i, 0)),
            pl.BlockSpec((tm, LANES), lambda i, k: (i, 0)),
        ],
        out_shape=[
            jax.ShapeDtypeStruct((m, d), F32),
            jax.ShapeDtypeStruct((m, d // 2), U32),
            jax.ShapeDtypeStruct((m, LANES), F32),
        ],
        compiler_params=_params("parallel", "arbitrary"),
        name="out_proj",
    )(hm, oh, w_out_b, hp, gain, w_router, b_router)


def _group_weight_pipeline(s, sf_ref, sg_ref, ge_ref, gj_ref, nu_ref, copies, cast):
    @pl.when(jnp.logical_and(s < nu_ref[0], sf_ref[s] == 1))
    def _():
        g = sg_ref[s]
        slot = g % 2

        @pl.when(g == 0)
        def _():
            for cp in copies(ge_ref[0], gj_ref[0], 0):
                cp.start()

        for cp in copies(ge_ref[g], gj_ref[g], slot):
            cp.wait()

        @pl.when(g + 1 < nu_ref[1])
        def _():
            for cp in copies(ge_ref[g + 1], gj_ref[g + 1], 1 - slot):
                cp.start()

        cast(slot)


def _moe_up_kernel(st_ref, se_ref, sj_ref, sf_ref, sg_ref, ge_ref, gj_ref, nu_ref, x_ref, w_hbm, bg_ref,
                   bl_ref, act_ref, wbuf, wgb, wlb, sem, *, n_up):
    s = pl.program_id(0)
    valid = s < nu_ref[0]
    tn = wgb.shape[1]

    def copies(e, j, slot):
        return [pltpu.make_async_copy(w_hbm.at[e, :, pl.ds(pl.multiple_of((part * n_up + j) * tn, tn), tn)],
                                      wbuf.at[slot, part], sem.at[slot, part]) for part in range(2)]

    def cast(slot):
        wgb[...] = wbuf[slot, 0].astype(BF16)
        wlb[...] = wbuf[slot, 1].astype(BF16)

    _group_weight_pipeline(s, sf_ref, sg_ref, ge_ref, gj_ref, nu_ref, copies, cast)

    @pl.when(valid)
    def _():
        half = wgb.shape[0] // 2
        lo, hi = _unpack_bf16_pair(x_ref[...])
        lo = lo.astype(BF16)
        hi = hi.astype(BF16)
        gate = _dot(lo, wgb[:half, :]) + _dot(hi, wgb[half:, :]) + bg_ref[...]
        lin = _dot(lo, wlb[:half, :]) + _dot(hi, wlb[half:, :]) + bl_ref[...]
        gate = jnp.minimum(gate, SWIGLU_LIMIT)
        lin = jnp.clip(lin, -SWIGLU_LIMIT, SWIGLU_LIMIT)
        act_ref[...] = (gate * _sigmoid(SWIGLU_ALPHA * gate) * (lin + 1.0)).astype(act_ref.dtype)

    @pl.when(jnp.logical_not(valid))
    def _():
        act_ref[...] = jnp.zeros_like(act_ref)


def _moe_down_kernel(st_ref, se_ref, sj_ref, sf_ref, sg_ref, ge_ref, gj_ref, nu_ref, a_ref, w_hbm, bd_ref,
                     y_ref, wbuf, wdb, sem):
    s = pl.program_id(0)
    valid = s < nu_ref[0]
    tn = wdb.shape[1]

    def copies(e, j, slot):
        return [pltpu.make_async_copy(w_hbm.at[e, :, pl.ds(pl.multiple_of(j * tn, tn), tn)],
                                      wbuf.at[slot], sem.at[slot])]

    def cast(slot):
        wdb[...] = wbuf[slot].astype(BF16)

    _group_weight_pipeline(s, sf_ref, sg_ref, ge_ref, gj_ref, nu_ref, copies, cast)

    @pl.when(valid)
    def _():
        y = _dot(a_ref[...], wdb[...]) + bd_ref[...]
        half = y.shape[1] // 2
        y_ref[...] = _pack_bf16_pair(y[:, :half], y[:, half:])

    @pl.when(jnp.logical_not(valid))
    def _():
        y_ref[...] = jnp.zeros_like(y_ref)


def _moe_schedule(tile_e, tile_start, tile_count, n_used, n_blocks, n_tiles):
    s = jnp.arange(n_blocks * n_tiles, dtype=jnp.int32)
    used = s < n_used * n_blocks
    e = jnp.repeat(tile_e, n_blocks)
    onehot = e[:, None] == jnp.arange(tile_start.shape[0], dtype=jnp.int32)[None, :]
    start_e = jnp.sum(jnp.where(onehot, tile_start[None, :], 0), axis=1)
    nt = jnp.maximum(jnp.sum(jnp.where(onehot, tile_count[None, :], 0), axis=1), 1)
    local = s - n_blocks * start_e
    sj = jnp.where(used, local // nt, s % n_blocks).astype(jnp.int32)
    st = jnp.where(used, start_e + local % nt, s // n_blocks).astype(jnp.int32)
    sf = jnp.logical_and(used, local % nt == 0).astype(jnp.int32)
    n_exp = tile_start.shape[0]
    nonempty = (tile_count > 0).astype(jnp.int32)
    ne_cum = jnp.cumsum(nonempty)
    ne_rank = ne_cum - nonempty
    sg = (jnp.sum(jnp.where(onehot, ne_rank[None, :], 0), axis=1) * n_blocks + sj).astype(jnp.int32)
    gi = jnp.arange(n_exp * n_blocks, dtype=jnp.int32)
    ge = jnp.minimum(jnp.sum((ne_cum[None, :] <= (gi // n_blocks)[:, None]).astype(jnp.int32), axis=1),
                     n_exp - 1).astype(jnp.int32)
    gj = (gi % n_blocks).astype(jnp.int32)
    nu = jnp.stack([n_used * n_blocks, ne_cum[-1] * n_blocks]).astype(jnp.int32)
    return st, e.astype(jnp.int32), sj, sf, sg, ge, gj, nu


def _clamped_step(s, nu):
    return jnp.minimum(s, nu[0] - 1)


def _moe_up(sched, x_sorted, w_gu, b_gu, tn_up):
    p, half = x_sorted.shape
    d = 2 * half
    ff = w_gu.shape[2] // 2
    tm = MOE_TILE
    n_up = ff // tn_up
    n_steps = sched[0].shape[0]
    cl = _clamped_step

    grid_spec = pltpu.PrefetchScalarGridSpec(
        num_scalar_prefetch=8,
        grid=(n_steps,),
        in_specs=[
            pl.BlockSpec((tm, half), lambda s, st, se, sj, sf, sg, ge, gj, nu: (st[cl(s, nu)], 0)),
            pl.BlockSpec(memory_space=pl.ANY),
            pl.BlockSpec((None, 1, tn_up),
                         lambda s, st, se, sj, sf, sg, ge, gj, nu: (se[cl(s, nu)], 0, sj[cl(s, nu)])),
            pl.BlockSpec((None, 1, tn_up),
                         lambda s, st, se, sj, sf, sg, ge, gj, nu: (se[cl(s, nu)], 0, n_up + sj[cl(s, nu)])),
        ],
        out_specs=pl.BlockSpec((tm, tn_up), lambda s, st, se, sj, sf, sg, ge, gj, nu: (st[s], sj[s])),
        scratch_shapes=[pltpu.VMEM((2, 2, d, tn_up), w_gu.dtype), pltpu.VMEM((d, tn_up), BF16),
                        pltpu.VMEM((d, tn_up), BF16), pltpu.SemaphoreType.DMA((2, 2))],
    )
    return pl.pallas_call(
        functools.partial(_moe_up_kernel, n_up=n_up),
        grid_spec=grid_spec,
        out_shape=jax.ShapeDtypeStruct((p, ff), BF16),
        compiler_params=_params("arbitrary"),
        name="moe_up",
    )(*sched, x_sorted, w_gu, b_gu, b_gu)


def _moe_down(sched, act, w_dn, b_dn, tn_dn):
    p, ff = act.shape
    d = w_dn.shape[2]
    tm = MOE_TILE
    n_steps = sched[0].shape[0]
    cl = _clamped_step

    grid_spec = pltpu.PrefetchScalarGridSpec(
        num_scalar_prefetch=8,
        grid=(n_steps,),
        in_specs=[
            pl.BlockSpec((tm, ff), lambda s, st, se, sj, sf, sg, ge, gj, nu: (st[cl(s, nu)], 0)),
            pl.BlockSpec(memory_space=pl.ANY),
            pl.BlockSpec((None, 1, tn_dn),
                         lambda s, st, se, sj, sf, sg, ge, gj, nu: (se[cl(s, nu)], 0, sj[cl(s, nu)])),
        ],
        out_specs=pl.BlockSpec((tm, tn_dn // 2), lambda s, st, se, sj, sf, sg, ge, gj, nu: (st[s], sj[s])),
        scratch_shapes=[pltpu.VMEM((2, ff, tn_dn), w_dn.dtype), pltpu.VMEM((ff, tn_dn), BF16),
                        pltpu.SemaphoreType.DMA((2,))],
    )
    return pl.pallas_call(
        _moe_down_kernel,
        grid_spec=grid_spec,
        out_shape=jax.ShapeDtypeStruct((p, d // 2), U32),
        compiler_params=_params("arbitrary"),
        name="moe_down",
    )(*sched, act, w_dn, b_dn)


def _row_copy(src, src_row, dst, dst_row, sem, rows=1):
    return pltpu.make_async_copy(src.at[pl.ds(src_row, rows)], dst.at[pl.ds(dst_row, rows)], sem)


def _dispatch_kernel(dest_ref, zlo_ref, zhi_ref, nused_ref, x_ref, xs_hbm, zbuf, sem_tok, sem_zero,
                     *, nblk, t_tok, n_exp, n_tiles, tm):
    i = pl.program_id(0)
    b = i // nblk
    j = i % nblk
    rows = x_ref.shape[0]

    def zero_rows(wait):
        def per_expert(e, c):
            def body(p, c2):
                cp = _row_copy(zbuf, 0, xs_hbm, p, sem_zero)
                cp.wait() if wait else cp.start()
                return c2
            return lax.fori_loop(zlo_ref[e], zhi_ref[e], body, c)
        lax.fori_loop(0, n_exp, per_expert, 0)

    def zero_tiles(wait):
        def body(t, c):
            cp = _row_copy(zbuf, 0, xs_hbm, pl.multiple_of(t * tm, tm), sem_zero, rows=tm)
            cp.wait() if wait else cp.start()
            return c
        lax.fori_loop(nused_ref[0], n_tiles, body, 0)

    @pl.when(i == 0)
    def _():
        zbuf[...] = jnp.zeros_like(zbuf)
        zero_rows(False)
        zero_tiles(False)

    first = jnp.where(j == 0, FRONT_PAD, 0)
    tok0 = b * t_tok + j * rows - FRONT_PAD

    def tok_body(r, c):
        a0 = (tok0 + r) * TOP_K
        for k in range(TOP_K):
            _row_copy(x_ref, r, xs_hbm, dest_ref[a0 + k], sem_tok).start()
        return c
    lax.fori_loop(first, rows, tok_body, 0)

    @pl.when(j == 0)
    def _():
        def wait_body(r, c):
            for k in range(TOP_K):
                _row_copy(x_ref, 0, xs_hbm, 0, sem_tok).wait()
            return c
        lax.fori_loop(FRONT_PAD, rows, wait_body, 0)

    @pl.when(j != 0)
    def _():
        for k in range(TOP_K):
            _row_copy(x_ref, 0, xs_hbm, 0, sem_tok, rows=rows).wait()

    @pl.when(i == 0)
    def _():
        zero_rows(True)
        zero_tiles(True)


def _dispatch(dest, zlo, zhi, n_used, u2p, bsz, n, t_tok, n_tiles):
    half = u2p.shape[1]
    tm = MOE_TILE
    rows = CHUNK
    nblk = n // rows
    kern = functools.partial(_dispatch_kernel, nblk=nblk, t_tok=t_tok, n_exp=zlo.shape[0],
                             n_tiles=n_tiles, tm=tm)
    grid_spec = pltpu.PrefetchScalarGridSpec(
        num_scalar_prefetch=4,
        grid=(bsz * nblk,),
        in_specs=[pl.BlockSpec((rows, half), lambda i, de, lo, hi, nu: (i, 0))],
        out_specs=pl.BlockSpec(memory_space=pl.ANY),
        scratch_shapes=[pltpu.VMEM((tm, half), U32), pltpu.SemaphoreType.DMA(()), pltpu.SemaphoreType.DMA(())],
    )
    return pl.pallas_call(
        kern,
        grid_spec=grid_spec,
        out_shape=jax.ShapeDtypeStruct((n_tiles * tm, half), U32),
        compiler_params=_params("arbitrary"),
        name="moe_dispatch",
    )(dest, zlo, zhi, n_used, u2p)


def _final_kernel(dest_ref, h1_ref, g_ref, gain_ref, y_hbm, out_ref, ybuf, sem, *, tm, n_tiles, tn_dn):
    i = pl.program_id(0)

    def gather(tile, slot, wait):
        base = tile * (tm * TOP_K)
        if wait:
            for k in range(TOP_K):
                pltpu.make_async_copy(y_hbm.at[pl.ds(0, tm)], ybuf.at[slot, k], sem.at[slot]).wait()
            return

        def body(r, c):
            for k in range(TOP_K):
                pltpu.make_async_copy(y_hbm.at[pl.ds(dest_ref[base + r * TOP_K + k], 1)],
                                      ybuf.at[slot, k, pl.ds(r, 1)], sem.at[slot]).start()
            return c
        lax.fori_loop(0, tm, body, 0)

    @pl.when(i == 0)
    def _():
        gather(0, 0, False)

    @pl.when(i + 1 < n_tiles)
    def _():
        gather(i + 1, (i + 1) % 2, False)

    slot = i % 2
    gather(i, slot, True)

    h = h1_ref[...]
    g = g_ref[...]
    hw = tn_dn // 2
    for k in range(TOP_K):
        w = ybuf[slot, k]
        parts = []
        for jb in range(w.shape[1] // hw):
            lo, hi = _unpack_bf16_pair(w[:, jb * hw:(jb + 1) * hw])
            parts += [lo, hi]
        h = h + g[:, k:k + 1] * jnp.concatenate(parts, axis=1)
    out_ref[...] = h * lax.rsqrt(jnp.mean(h * h, axis=-1, keepdims=True) + NORM_EPS) * gain_ref[...]


def _final(dest_seq, h1_3, gates_seq, gain, y_sorted, seq, tn_dn):
    b, n, d = h1_3.shape
    tm = CHUNK
    skip = (n - seq) // tm
    nblk = seq // tm
    n_tiles = b * nblk
    kern = functools.partial(_final_kernel, tm=tm, n_tiles=n_tiles, tn_dn=tn_dn)
    grid_spec = pltpu.PrefetchScalarGridSpec(
        num_scalar_prefetch=1,
        grid=(n_tiles,),
        in_specs=[
            pl.BlockSpec((None, tm, d), lambda i, ds: (i // nblk, skip + i % nblk, 0)),
            pl.BlockSpec((tm, TOP_K), lambda i, ds: (i, 0)),
            pl.BlockSpec((1, d), lambda i, ds: (0, 0)),
            pl.BlockSpec(memory_space=pl.ANY),
        ],
        out_specs=pl.BlockSpec((None, tm, d), lambda i, ds: (i // nblk, i % nblk, 0)),
        scratch_shapes=[pltpu.VMEM((2, TOP_K, tm, d // 2), U32), pltpu.SemaphoreType.DMA((2,))],
    )
    return pl.pallas_call(
        kern,
        grid_spec=grid_spec,
        out_shape=jax.ShapeDtypeStruct((b, seq, d), F32),
        compiler_params=_params("arbitrary"),
        name="final_norm",
    )(dest_seq, h1_3, gates_seq, gain, y_sorted)


def kernel(x, meta_tokens, norm_mix, w_in, b_mlstm_gates, mlstm_head_norm, hgrn_lower_bound,
           hgrn_head_norm, w_out, norm_ffn, w_router, b_router, w_gate_up, b_gate_up, w_down,
           b_down, norm_final):
    bsz, seq, d = x.shape
    depth = w_in.shape[0]
    assert depth == 1 and seq % CHUNK == 0
    mh = b_mlstm_gates.shape[-1] // 2
    mw = mh * MLSTM_DV
    hw = hgrn_lower_bound.shape[-1]
    hh = hw // HGRN_DK
    n_exp = w_router.shape[-1]
    n = CHUNK + seq
    t_tok = NUM_META + seq

    meta = jnp.broadcast_to(meta_tokens[None].astype(x.dtype), (bsz, NUM_META, d))
    hp = jnp.concatenate([jnp.zeros((bsz, FRONT_PAD, d), x.dtype), meta, x], axis=1)
    hp2 = hp.reshape(bsz * n, d)

    g0 = 2 * mh * MLSTM_DQK + 2 * mw
    w_in0 = w_in[0]
    w_main = jnp.concatenate([w_in0[:, :g0], w_in0[:, g0 + 2 * mh:]], axis=1).astype(BF16)
    w_gates = jnp.pad(w_in0[:, g0:g0 + 2 * mh], ((0, 0), (0, GATE_LANES - 2 * mh))).astype(BF16)
    bias_col = jnp.pad(b_mlstm_gates[0].astype(F32), (0, GATE_LANES - 2 * mh)).reshape(1, GATE_LANES)
    bias_row = b_mlstm_gates[0].astype(F32).reshape(2 * mh, 1)
    lbs = jnp.cumsum(jax.nn.softmax(hgrn_lower_bound.astype(F32), axis=0), axis=0)[0].reshape(1, hw)

    z, gates = _in_proj(hp2, norm_mix[0].reshape(1, d), w_main, w_gates)
    z3 = z.reshape(bsz, n, -1)
    gates_col = gates.reshape(bsz, n, GATE_LANES)
    gates_row = jnp.swapaxes(gates_col[:, :, :2 * mh], 1, 2)
    hm = _mlstm(z3, gates_col, gates_row, bias_col, bias_row, mlstm_head_norm[0].reshape(1, mw), mh)
    oh = _hgrn(z3, lbs, hgrn_head_norm[0].reshape(1, hw), g0, hh)

    wr = jnp.pad(w_router[0].astype(F32), ((0, 0), (0, LANES - n_exp)))
    br = jnp.pad(b_router[0].astype(F32), (0, LANES - n_exp)).reshape(1, LANES)
    h1, u2, logits = _out_proj(hm.reshape(bsz * n, mw), oh.reshape(bsz * n, hw), w_out[0].astype(BF16),
                               hp2, norm_ffn[0].reshape(1, d), wr, br)

    t_all = bsz * t_tok
    tk = t_all * TOP_K
    logits_tok = logits.reshape(bsz, n, LANES)[:, FRONT_PAD:, :n_exp].reshape(t_all, n_exp)
    top_val, top_idx = lax.top_k(logits_tok, TOP_K)
    gate_w = jax.nn.softmax(top_val, axis=-1)
    flat_e = top_idx.reshape(-1).astype(jnp.int32)
    onehot = (flat_e[:, None] == jnp.arange(n_exp, dtype=jnp.int32)[None, :]).astype(jnp.int32)
    csum = jnp.cumsum(onehot, axis=0)
    rank = jnp.sum((csum - onehot) * onehot, axis=1)
    counts = csum[-1]
    tm = MOE_TILE
    padded = (counts + tm - 1) // tm * tm
    pad_end = jnp.cumsum(padded)
    pad_start = pad_end - padded
    dest = (jnp.sum(onehot * pad_start[None, :], axis=1) + rank).astype(jnp.int32)
    n_tiles = -(-(tk + n_exp * (tm - 1)) // tm)
    tile_row0 = jnp.arange(n_tiles, dtype=jnp.int32) * tm
    tile_e = jnp.minimum(jnp.sum((tile_row0[:, None] >= pad_end[None, :]).astype(jnp.int32), axis=1),
                         n_exp - 1).astype(jnp.int32)
    n_used = (pad_end[-1] // tm).astype(jnp.int32)

    ff = w_down.shape[2]
    tn_up = _largest_tile(ff, 512)
    tn_dn = _largest_tile(d, 2048)
    tile_start = (pad_start // tm).astype(jnp.int32)
    tile_count = (padded // tm).astype(jnp.int32)
    sched_up = _moe_schedule(tile_e, tile_start, tile_count, n_used, ff // tn_up, n_tiles)
    sched_dn = _moe_schedule(tile_e, tile_start, tile_count, n_used, d // tn_dn, n_tiles)

    x_sorted = _dispatch(dest, (pad_start + counts).astype(jnp.int32), pad_end.astype(jnp.int32),
                         n_used.reshape(1), u2, bsz, n, t_tok, n_tiles)
    act = _moe_up(sched_up, x_sorted, w_gate_up[0], b_gate_up[0].astype(F32).reshape(n_exp, 1, -1), tn_up)
    y_sorted = _moe_down(sched_dn, act, w_down[0], b_down[0].astype(F32).reshape(n_exp, 1, d), tn_dn)

    dest_seq = dest.reshape(bsz, t_tok, TOP_K)[:, NUM_META:, :].reshape(-1)
    gates_seq = gate_w.reshape(bsz, t_tok, TOP_K)[:, NUM_META:, :].reshape(bsz * seq, TOP_K)
    return _final(dest_seq, h1.reshape(bsz, n, d), gates_seq, norm_final.reshape(1, d), y_sorted, seq, tn_dn)
```

```python
import functools

import jax
import jax.numpy as jnp
from jax import lax
from jax.experimental import pallas as pl
from jax.experimental.pallas import tpu as pltpu

F32 = jnp.float32
BF16 = jnp.bfloat16
U32 = jnp.uint32

NUM_META = 16
MLSTM_DV = 512
MLSTM_DQK = 256
HGRN_DK = 128
GATE_SOFTCAP = 15.0
TOP_K = 4
SWIGLU_LIMIT = 7.0
SWIGLU_ALPHA = 1.702
NORM_EPS = 1e-5
HEAD_NORM_EPS = 1e-6
LOG2_E = 1.4426950408889634

LANES = 128
VMEM_LIMIT_BYTES = 58 * 1024 * 1024

CHUNK = 256
FRONT_PAD = CHUNK - NUM_META
GATE_LANES = LANES

MOE_TILE = 512

def _largest_tile(total, target):
    best = LANES
    t = LANES
    while t <= min(total, target):
        if total % t == 0:
            best = t
        t += LANES
    assert total % best == 0, (total, target)
    return best


def _log_sigmoid(x):
    return jnp.minimum(x, 0.0) - jnp.log1p(jnp.exp(-jnp.abs(x)))


def _sigmoid(x):
    return 1.0 / (1.0 + jnp.exp(-x))


def _pack_bf16_pair(lo, hi):
    lo_bits = lax.bitcast_convert_type(lo.astype(BF16).astype(F32), U32) >> 16
    hi_bits = lax.bitcast_convert_type(hi.astype(BF16).astype(F32), U32) & jnp.uint32(0xFFFF0000)
    return lo_bits | hi_bits


def _unpack_bf16_pair(w):
    lo = lax.bitcast_convert_type(w << 16, F32)
    hi = lax.bitcast_convert_type(w & jnp.uint32(0xFFFF0000), F32)
    return lo, hi


def _dot(a, b):
    return jnp.dot(a, b, preferred_element_type=F32)


def _dot_nt(a, b):
    return lax.dot_general(a, b, (((1,), (1,)), ((), ())), preferred_element_type=F32)


def _dot_tn(a, b):
    return lax.dot_general(a, b, (((0,), (0,)), ((), ())), preferred_element_type=F32)


def _params(*sem):
    return pltpu.CompilerParams(dimension_semantics=sem, vmem_limit_bytes=VMEM_LIMIT_BYTES)


def _inproj_kernel(h_ref, gain_ref, wg_ref, w_ref, z_ref, g_ref, u_scr):
    @pl.when(pl.program_id(1) == 0)
    def _():
        x = h_ref[...]
        ms = jnp.mean(x * x, axis=-1, keepdims=True)
        u = (x * lax.rsqrt(ms + NORM_EPS) * gain_ref[...]).astype(BF16)
        u_scr[...] = u
        g_ref[...] = _dot(u, wg_ref[...])

    z_ref[...] = _dot(u_scr[...], w_ref[...]).astype(z_ref.dtype)


def _in_proj(hp, gain, w_main, w_gates):
    m, d = hp.shape
    zw = w_main.shape[1]
    tm = _largest_tile(m, 512)
    tn = _largest_tile(zw, 1024)
    return pl.pallas_call(
        _inproj_kernel,
        grid=(m // tm, zw // tn),
        in_specs=[
            pl.BlockSpec((tm, d), lambda i, j: (i, 0)),
            pl.BlockSpec((1, d), lambda i, j: (0, 0)),
            pl.BlockSpec((d, GATE_LANES), lambda i, j: (0, 0)),
            pl.BlockSpec((d, tn), lambda i, j: (0, j)),
        ],
        out_specs=[
            pl.BlockSpec((tm, tn), lambda i, j: (i, j)),
            pl.BlockSpec((tm, GATE_LANES), lambda i, j: (i, 0)),
        ],
        out_shape=[
            jax.ShapeDtypeStruct((m, zw), BF16),
            jax.ShapeDtypeStruct((m, GATE_LANES), F32),
        ],
        scratch_shapes=[pltpu.VMEM((tm, d), BF16)],
        compiler_params=_params("parallel", "arbitrary"),
        name="in_proj",
    )(hp, gain, w_gates, w_main)


def _mlstm_kernel(q_ref, k_ref, v_ref, o_ref, gc_ref, gr_ref, bc_ref, br_ref, gain_ref, out_ref,
                  ct, nst, mst, *, heads, chunk):
    hd = pl.program_id(1)
    c = pl.program_id(2)
    L = chunk
    neg_inf = -jnp.inf

    @pl.when(c == 0)
    def _():
        ct[...] = jnp.zeros_like(ct)
        nst[...] = jnp.zeros_like(nst)
        mst[...] = jnp.zeros_like(mst)

    pos_col = lax.broadcasted_iota(jnp.int32, (L, 1), 0) + c * L
    pos_row = lax.broadcasted_iota(jnp.int32, (1, L), 1) + c * L
    pad_col = pos_col < FRONT_PAD
    pad_row = pos_row < FRONT_PAD

    gcol = gc_ref[...] + bc_ref[...]
    gcol = GATE_SOFTCAP * jnp.tanh(gcol / GATE_SOFTCAP)
    lane = lax.broadcasted_iota(jnp.int32, gcol.shape, 1)
    lf_col_all = jnp.where(pad_col, 0.0, _log_sigmoid(gcol))
    li_col = jnp.sum(jnp.where(lane == hd, gcol, 0.0), axis=1, keepdims=True)
    li_col = jnp.where(pad_col, neg_inf, li_col)

    grow = gr_ref[...] + br_ref[...]
    grow = GATE_SOFTCAP * jnp.tanh(grow / GATE_SOFTCAP)
    sub = lax.broadcasted_iota(jnp.int32, grow.shape, 0)
    lf_row_all = jnp.where(pad_row, 0.0, _log_sigmoid(grow))
    li_row = jnp.sum(jnp.where(sub == hd, grow, 0.0), axis=0, keepdims=True)
    li_row = jnp.where(pad_row, neg_inf, li_row)

    r_i = lax.broadcasted_iota(jnp.int32, (L, L), 0)
    c_i = lax.broadcasted_iota(jnp.int32, (L, L), 1)
    causal = r_i >= c_i
    lower = causal.astype(BF16)
    upper = (r_i <= c_i).astype(BF16)
    c_hi = lf_col_all.astype(BF16)
    c_lo = (lf_col_all - c_hi.astype(F32)).astype(BF16)
    a_col_all = _dot(lower, c_hi) + _dot(lower, c_lo)
    a_col = jnp.sum(jnp.where(lane == heads + hd, a_col_all, 0.0), axis=1, keepdims=True)
    r_hi = lf_row_all.astype(BF16)
    r_lo = (lf_row_all - r_hi.astype(F32)).astype(BF16)
    a_row_all = _dot(r_hi, upper) + _dot(r_lo, upper)
    a_row = jnp.sum(jnp.where(sub == heads + hd, a_row_all, 0.0), axis=0, keepdims=True)

    m_prev = mst[...]
    inter = a_col + m_prev
    d_log = jnp.where(causal, a_col - a_row + li_row, neg_inf)
    m_row = jnp.maximum(inter, jnp.max(d_log, axis=1, keepdims=True))

    q = q_ref[...]
    kf = k_ref[...].astype(F32) * (MLSTM_DQK ** -0.5)
    kb = kf.astype(BF16)
    v = v_ref[...]
    scores = _dot_nt(q, kb) * jnp.exp(d_log - m_row)
    inter_w = jnp.exp(inter - m_row)
    num = _dot(scores.astype(BF16), v) + inter_w * _dot_nt(q, ct[...].astype(BF16))
    den = jnp.sum(scores, axis=1, keepdims=True) + inter_w * jnp.sum(
        q.astype(F32) * nst[...], axis=1, keepdims=True)
    h_out = num / jnp.maximum(jnp.abs(den), jnp.exp(-m_row))

    y = h_out * lax.rsqrt(jnp.mean(h_out * h_out, axis=-1, keepdims=True) + HEAD_NORM_EPS)
    y = y * gain_ref[...] * _sigmoid(o_ref[...].astype(F32))
    out_ref[...] = y.astype(out_ref.dtype)

    a_end = a_col[L - 1:L, :]
    dec_col = a_end - a_col + li_col
    dec_row = a_end - a_row + li_row
    m_new = jnp.maximum(a_end + m_prev, jnp.max(dec_row, axis=1, keepdims=True))
    w_col = jnp.exp(dec_col - m_new)
    carry_w = jnp.exp(a_end + m_prev - m_new)
    kw = kf * w_col
    ct[...] = carry_w * ct[...] + _dot_tn(v, kw.astype(BF16))
    nst[...] = carry_w * nst[...] + jnp.sum(kw, axis=0, keepdims=True)
    mst[...] = m_new


def _mlstm(z3, gates_col, gates_row, bias_col, bias_row, gain, heads):
    b, n, _ = z3.shape
    L = CHUNK
    nc = n // L
    dq, dv = MLSTM_DQK, MLSTM_DV
    assert dv == 2 * dq
    kern = functools.partial(_mlstm_kernel, heads=heads, chunk=L)
    return pl.pallas_call(
        kern,
        grid=(b, heads, nc),
        in_specs=[
            pl.BlockSpec((None, L, dq), lambda i, h, c: (i, c, h)),
            pl.BlockSpec((None, L, dq), lambda i, h, c: (i, c, heads + h)),
            pl.BlockSpec((None, L, dv), lambda i, h, c: (i, c, heads + h)),
            pl.BlockSpec((None, L, dv), lambda i, h, c: (i, c, 2 * heads + h)),
            pl.BlockSpec((None, L, GATE_LANES), lambda i, h, c: (i, c, 0)),
            pl.BlockSpec((None, 2 * heads, L), lambda i, h, c: (i, 0, c)),
            pl.BlockSpec((1, GATE_LANES), lambda i, h, c: (0, 0)),
            pl.BlockSpec((2 * heads, 1), lambda i, h, c: (0, 0)),
            pl.BlockSpec((1, dv), lambda i, h, c: (0, h)),
        ],
        out_specs=pl.BlockSpec((None, L, dv), lambda i, h, c: (i, c, h)),
        out_shape=jax.ShapeDtypeStruct((b, n, heads * dv), BF16),
        scratch_shapes=[pltpu.VMEM((dv, dq), F32), pltpu.VMEM((1, dq), F32), pltpu.VMEM((1, 1), F32)],
        compiler_params=_params("parallel", "parallel", "arbitrary"),
        name="mlstm",
    )(z3, z3, z3, z3, gates_col, gates_row, bias_col, bias_row, gain)


def _block_ref_rows(g, hs, L):
    d = g.shape[1]
    if hs >= 4:
        blk = 2 * hs
        g3 = g.reshape(L // blk, blk, d)
        return jnp.broadcast_to(g3[:, hs - 1:hs, :], g3.shape).reshape(L, d)
    row = lax.broadcasted_iota(jnp.int32, (L, 1), 0)
    if hs == 2:
        r = row & 3
        up1 = pltpu.roll(g, L - 1, 0)
        dn1 = pltpu.roll(g, 1, 0)
        dn2 = pltpu.roll(g, 2, 0)
        return jnp.where(r == 0, up1, jnp.where(r == 1, g, jnp.where(r == 2, dn1, dn2)))
    assert hs == 1
    return jnp.where((row & 1) == 1, pltpu.roll(g, 1, 0), g)


def _hgrn_kernel(q_ref, f_ref, i_ref, g_ref, lb_ref, gain_ref, tri_ref, lvl_ref, out_ref, st,
                 *, chunk, heads_per_step):
    c = pl.program_id(2)
    L = chunk
    dk = HGRN_DK

    @pl.when(c == 0)
    def _():
        st[...] = jnp.zeros_like(st)

    row = lax.broadcasted_iota(jnp.int32, (L, 1), 0)
    is_pad = (row + c * L) < FRONT_PAD
    tri = tri_ref[...]

    for h in range(heads_per_step):
        cols = slice(h * dk, (h + 1) * dk)
        lb = lb_ref[:, cols]
        f_pre = f_ref[:, cols].astype(F32)
        t0 = jnp.log(lb)
        t1 = jnp.log1p(-lb) + _log_sigmoid(f_pre)
        log_f = jnp.maximum(t0, t1) + jnp.log1p(jnp.exp(-jnp.abs(t0 - t1)))
        kh = (1.0 - lb) * _sigmoid(-f_pre)
        log_f = jnp.where(is_pad, 0.0, log_f)
        kh = jnp.where(is_pad, 0.0, kh)
        qf = q_ref[:, cols].astype(F32)
        qh = qf * _sigmoid(qf) * (HGRN_DK ** -0.5)
        v = i_ref[:, cols]

        f_hi = log_f.astype(BF16)
        f_lo = (log_f - f_hi.astype(F32)).astype(BF16)
        g2 = _dot(tri, jnp.concatenate([f_hi, f_lo], axis=1))
        g_cum = g2[:, :dk] + g2[:, dk:]
        g_log2 = g_cum * LOG2_E

        o = _dot_nt((qh * jnp.exp2(g_log2)).astype(BF16), st[h].astype(BF16))

        diag = jnp.sum(qh * kh, axis=1, keepdims=True)
        attn = jnp.where(lvl_ref[...] == 0, diag, 0.0)
        hs = 1
        while hs < L:
            dj = g_log2 - _block_ref_rows(g_log2, hs, L)
            second = (row & hs) != 0
            qj = (qh * jnp.exp2(jnp.where(second, dj, -jnp.inf))).astype(BF16)
            kj = (kh * jnp.exp2(jnp.where(second, -jnp.inf, -dj))).astype(BF16)
            attn = jnp.where(lvl_ref[...] >= hs, _dot_nt(qj, kj), attn)
            hs *= 2
        o = o + _dot(attn.astype(BF16), v)

        y = o * lax.rsqrt(jnp.mean(o * o, axis=-1, keepdims=True) + HEAD_NORM_EPS)
        gate = g_ref[:, cols].astype(F32)
        y = y * gain_ref[:, cols] * (gate * _sigmoid(gate))
        out_ref[:, cols] = y.astype(out_ref.dtype)

        g_end = g_log2[L - 1:L, :]
        k_dec = (kh * jnp.exp2(g_end - g_log2)).astype(BF16)
        st[h] = jnp.exp2(g_end) * st[h] + _dot_tn(v, k_dec)


def _hgrn(z3, lb, gain, col0, heads):
    b, n, _ = z3.shape
    L = CHUNK
    nc = n // L
    dk = HGRN_DK
    hb = 2 if (heads % 2 == 0 and (col0 // dk) % 2 == 0) else 1
    wb = hb * dk
    base = col0 // wb
    kern = functools.partial(_hgrn_kernel, chunk=L, heads_per_step=hb)
    r_i = lax.broadcasted_iota(jnp.int32, (L, L), 0)
    c_i = lax.broadcasted_iota(jnp.int32, (L, L), 1)
    tri = (r_i >= c_i).astype(BF16)
    lvl = r_i ^ c_i

    def zspec(k):
        return pl.BlockSpec((None, L, wb), lambda i, h, c: (i, c, base + k * (heads // hb) + h))

    return pl.pallas_call(
        kern,
        grid=(b, heads // hb, nc),
        in_specs=[zspec(0), zspec(1), zspec(2), zspec(3),
                  pl.BlockSpec((1, wb), lambda i, h, c: (0, h)),
                  pl.BlockSpec((1, wb), lambda i, h, c: (0, h)),
                  pl.BlockSpec((L, L), lambda i, h, c: (0, 0)),
                  pl.BlockSpec((L, L), lambda i, h, c: (0, 0))],
        out_specs=pl.BlockSpec((None, L, wb), lambda i, h, c: (i, c, h)),
        out_shape=jax.ShapeDtypeStruct((b, n, heads * dk), BF16),
        scratch_shapes=[pltpu.VMEM((hb, dk, dk), F32)],
        compiler_params=_params("parallel", "parallel", "arbitrary"),
        name="hgrn2",
    )(z3, z3, z3, z3, lb, gain, tri, lvl)


def _outproj_kernel(hm_ref, oh_ref, w_ref, res_ref, gain_ref, wrh_ref, wrl_ref, br_ref,
                    h1_ref, u2_ref, lg_ref, *, nk):
    k = pl.program_id(1)

    @pl.when(k == 0)
    def _():
        h1_ref[...] = res_ref[...]

    @pl.when(k < nk)
    def _():
        h1_ref[...] += _dot(hm_ref[...], w_ref[...])

    @pl.when(k >= nk)
    def _():
        h1_ref[...] += _dot(oh_ref[...], w_ref[...])

    @pl.when(k == pl.num_programs(1) - 1)
    def _():
        h1 = h1_ref[...]
        u2 = h1 * lax.rsqrt(jnp.mean(h1 * h1, axis=-1, keepdims=True) + NORM_EPS) * gain_ref[...]
        half = u2.shape[1] // 2
        u2_ref[...] = _pack_bf16_pair(u2[:, :half], u2[:, half:])
        u_hi = u2.astype(BF16)
        u_lo = (u2 - u_hi.astype(F32)).astype(BF16)
        lg_ref[...] = (_dot(u_hi, wrh_ref[...]) + _dot(u_lo, wrh_ref[...]) + _dot(u_hi, wrl_ref[...])
                       + br_ref[...])


def _out_proj(hm, oh, w_out_b, hp, gain, w_router_hi, w_router_lo, b_router):
    m, w1 = hm.shape
    w2 = oh.shape[1]
    d = w_out_b.shape[1]
    assert w1 == w2
    tm = _largest_tile(m, 384)
    tk = _largest_tile(w1, 1024)
    nk = w1 // tk
    return pl.pallas_call(
        functools.partial(_outproj_kernel, nk=nk),
        grid=(m // tm, 2 * nk),
        in_specs=[
            pl.BlockSpec((tm, tk), lambda i, k: (i, jnp.minimum(k, nk - 1))),
            pl.BlockSpec((tm, tk), lambda i, k: (i, jnp.maximum(k - nk, 0))),
            pl.BlockSpec((tk, d), lambda i, k: (k, 0)),
            pl.BlockSpec((tm, d), lambda i, k: (i, 0)),
            pl.BlockSpec((1, d), lambda i, k: (0, 0)),
            pl.BlockSpec((d, LANES), lambda i, k: (0, 0)),
            pl.BlockSpec((d, LANES), lambda i, k: (0, 0)),
            pl.BlockSpec((1, LANES), lambda i, k: (0, 0)),
        ],
        out_specs=[
            pl.BlockSpec((tm, d), lambda i, k: (i, 0)),
            pl.BlockSpec((tm, d // 2), lambda i, k: (i, 0)),
            pl.BlockSpec((tm, LANES), lambda i, k: (i, 0)),
        ],
        out_shape=[
            jax.ShapeDtypeStruct((m, d), F32),
            jax.ShapeDtypeStruct((m, d // 2), U32),
            jax.ShapeDtypeStruct((m, LANES), F32),
        ],
        compiler_params=_params("parallel", "arbitrary"),
        name="out_proj",
    )(hm, oh, w_out_b, hp, gain, w_router_hi, w_router_lo, b_router)


def _group_weight_pipeline(s, sf_ref, sg_ref, ge_ref, gj_ref, nu_ref, copies, cast):
    @pl.when(jnp.logical_and(s < nu_ref[0], sf_ref[s] == 1))
    def _():
        g = sg_ref[s]
        slot = g % 2

        @pl.when(g == 0)
        def _():
            for cp in copies(ge_ref[0], gj_ref[0], 0):
                cp.start()

        for cp in copies(ge_ref[g], gj_ref[g], slot):
            cp.wait()

        @pl.when(g + 1 < nu_ref[1])
        def _():
            for cp in copies(ge_ref[g + 1], gj_ref[g + 1], 1 - slot):
                cp.start()

        cast(slot)


def _moe_up_kernel(st_ref, se_ref, sj_ref, sf_ref, sg_ref, ge_ref, gj_ref, nu_ref, x_ref, w_hbm, bg_ref,
                   bl_ref, act_ref, wbuf, wgb, wlb, sem, *, n_up):
    s = pl.program_id(0)
    valid = s < nu_ref[0]
    tn = wgb.shape[1]

    def copies(e, j, slot):
        return [pltpu.make_async_copy(w_hbm.at[e, :, pl.ds(pl.multiple_of((part * n_up + j) * tn, tn), tn)],
                                      wbuf.at[slot, part], sem.at[slot, part]) for part in range(2)]

    def cast(slot):
        wgb[...] = wbuf[slot, 0].astype(BF16)
        wlb[...] = wbuf[slot, 1].astype(BF16)

    _group_weight_pipeline(s, sf_ref, sg_ref, ge_ref, gj_ref, nu_ref, copies, cast)

    @pl.when(valid)
    def _():
        half = wgb.shape[0] // 2
        lo, hi = _unpack_bf16_pair(x_ref[...])
        lo = lo.astype(BF16)
        hi = hi.astype(BF16)
        gate = _dot(lo, wgb[:half, :]) + _dot(hi, wgb[half:, :]) + bg_ref[...]
        lin = _dot(lo, wlb[:half, :]) + _dot(hi, wlb[half:, :]) + bl_ref[...]
        gate = jnp.minimum(gate, SWIGLU_LIMIT)
        lin = jnp.clip(lin, -SWIGLU_LIMIT, SWIGLU_LIMIT)
        act_ref[...] = (gate * _sigmoid(SWIGLU_ALPHA * gate) * (lin + 1.0)).astype(act_ref.dtype)

    @pl.when(jnp.logical_not(valid))
    def _():
        act_ref[...] = jnp.zeros_like(act_ref)


def _moe_down_kernel(st_ref, se_ref, sj_ref, sf_ref, sg_ref, ge_ref, gj_ref, nu_ref, a_ref, w_hbm, bd_ref,
                     y_ref, wbuf, wdb, sem):
    s = pl.program_id(0)
    valid = s < nu_ref[0]
    tn = wdb.shape[1]

    def copies(e, j, slot):
        return [pltpu.make_async_copy(w_hbm.at[e, :, pl.ds(pl.multiple_of(j * tn, tn), tn)],
                                      wbuf.at[slot], sem.at[slot])]

    def cast(slot):
        wdb[...] = wbuf[slot].astype(BF16)

    _group_weight_pipeline(s, sf_ref, sg_ref, ge_ref, gj_ref, nu_ref, copies, cast)

    @pl.when(valid)
    def _():
        y = _dot(a_ref[...], wdb[...]) + bd_ref[...]
        half = y.shape[1] // 2
        y_ref[...] = _pack_bf16_pair(y[:, :half], y[:, half:])

    @pl.when(jnp.logical_not(valid))
    def _():
        y_ref[...] = jnp.zeros_like(y_ref)


def _moe_schedule(tile_e, tile_start, tile_count, n_used, n_blocks, n_tiles):
    s = jnp.arange(n_blocks * n_tiles, dtype=jnp.int32)
    used = s < n_used * n_blocks
    e = jnp.repeat(tile_e, n_blocks)
    onehot = e[:, None] == jnp.arange(tile_start.shape[0], dtype=jnp.int32)[None, :]
    start_e = jnp.sum(jnp.where(onehot, tile_start[None, :], 0), axis=1)
    nt = jnp.maximum(jnp.sum(jnp.where(onehot, tile_count[None, :], 0), axis=1), 1)
    local = s - n_blocks * start_e
    sj = jnp.where(used, local // nt, s % n_blocks).astype(jnp.int32)
    st = jnp.where(used, start_e + local % nt, s // n_blocks).astype(jnp.int32)
    sf = jnp.logical_and(used, local % nt == 0).astype(jnp.int32)
    n_exp = tile_start.shape[0]
    nonempty = (tile_count > 0).astype(jnp.int32)
    ne_cum = jnp.cumsum(nonempty)
    ne_rank = ne_cum - nonempty
    sg = (jnp.sum(jnp.where(onehot, ne_rank[None, :], 0), axis=1) * n_blocks + sj).astype(jnp.int32)
    gi = jnp.arange(n_exp * n_blocks, dtype=jnp.int32)
    ge = jnp.minimum(jnp.sum((ne_cum[None, :] <= (gi // n_blocks)[:, None]).astype(jnp.int32), axis=1),
                     n_exp - 1).astype(jnp.int32)
    gj = (gi % n_blocks).astype(jnp.int32)
    nu = jnp.stack([n_used * n_blocks, ne_cum[-1] * n_blocks]).astype(jnp.int32)
    return st, e.astype(jnp.int32), sj, sf, sg, ge, gj, nu


def _clamped_step(s, nu):
    return jnp.minimum(s, nu[0] - 1)


def _moe_up(sched, x_sorted, w_gu, b_gu, tn_up):
    p, half = x_sorted.shape
    d = 2 * half
    ff = w_gu.shape[2] // 2
    tm = MOE_TILE
    n_up = ff // tn_up
    n_steps = sched[0].shape[0]
    cl = _clamped_step

    grid_spec = pltpu.PrefetchScalarGridSpec(
        num_scalar_prefetch=8,
        grid=(n_steps,),
        in_specs=[
            pl.BlockSpec((tm, half), lambda s, st, se, sj, sf, sg, ge, gj, nu: (st[cl(s, nu)], 0)),
            pl.BlockSpec(memory_space=pl.ANY),
            pl.BlockSpec((None, 1, tn_up),
                         lambda s, st, se, sj, sf, sg, ge, gj, nu: (se[cl(s, nu)], 0, sj[cl(s, nu)])),
            pl.BlockSpec((None, 1, tn_up),
                         lambda s, st, se, sj, sf, sg, ge, gj, nu: (se[cl(s, nu)], 0, n_up + sj[cl(s, nu)])),
        ],
        out_specs=pl.BlockSpec((tm, tn_up), lambda s, st, se, sj, sf, sg, ge, gj, nu: (st[s], sj[s])),
        scratch_shapes=[pltpu.VMEM((2, 2, d, tn_up), w_gu.dtype), pltpu.VMEM((d, tn_up), BF16),
                        pltpu.VMEM((d, tn_up), BF16), pltpu.SemaphoreType.DMA((2, 2))],
    )
    return pl.pallas_call(
        functools.partial(_moe_up_kernel, n_up=n_up),
        grid_spec=grid_spec,
        out_shape=jax.ShapeDtypeStruct((p, ff), BF16),
        compiler_params=_params("arbitrary"),
        name="moe_up",
    )(*sched, x_sorted, w_gu, b_gu, b_gu)


def _moe_down(sched, act, w_dn, b_dn, tn_dn):
    p, ff = act.shape
    d = w_dn.shape[2]
    tm = MOE_TILE
    n_steps = sched[0].shape[0]
    cl = _clamped_step

    grid_spec = pltpu.PrefetchScalarGridSpec(
        num_scalar_prefetch=8,
        grid=(n_steps,),
        in_specs=[
            pl.BlockSpec((tm, ff), lambda s, st, se, sj, sf, sg, ge, gj, nu: (st[cl(s, nu)], 0)),
            pl.BlockSpec(memory_space=pl.ANY),
            pl.BlockSpec((None, 1, tn_dn),
                         lambda s, st, se, sj, sf, sg, ge, gj, nu: (se[cl(s, nu)], 0, sj[cl(s, nu)])),
        ],
        out_specs=pl.BlockSpec((tm, tn_dn // 2), lambda s, st, se, sj, sf, sg, ge, gj, nu: (st[s], sj[s])),
        scratch_shapes=[pltpu.VMEM((2, ff, tn_dn), w_dn.dtype), pltpu.VMEM((ff, tn_dn), BF16),
                        pltpu.SemaphoreType.DMA((2,))],
    )
    return pl.pallas_call(
        _moe_down_kernel,
        grid_spec=grid_spec,
        out_shape=jax.ShapeDtypeStruct((p, d // 2), U32),
        compiler_params=_params("arbitrary"),
        name="moe_down",
    )(*sched, act, w_dn, b_dn)


def _row_copy(src, src_row, dst, dst_row, sem, rows=1):
    return pltpu.make_async_copy(src.at[pl.ds(src_row, rows)], dst.at[pl.ds(dst_row, rows)], sem)


def _dispatch_kernel(dest_ref, zlo_ref, zhi_ref, nused_ref, x_ref, xs_hbm, zbuf, sem_tok, sem_zero,
                     *, nblk, t_tok, n_exp, n_tiles, tm):
    i = pl.program_id(0)
    b = i // nblk
    j = i % nblk
    rows = x_ref.shape[0]

    def zero_rows(wait):
        def per_expert(e, c):
            def body(p, c2):
                cp = _row_copy(zbuf, 0, xs_hbm, p, sem_zero)
                cp.wait() if wait else cp.start()
                return c2
            return lax.fori_loop(zlo_ref[e], zhi_ref[e], body, c)
        lax.fori_loop(0, n_exp, per_expert, 0)

    def zero_tiles(wait):
        def body(t, c):
            cp = _row_copy(zbuf, 0, xs_hbm, pl.multiple_of(t * tm, tm), sem_zero, rows=tm)
            cp.wait() if wait else cp.start()
            return c
        lax.fori_loop(nused_ref[0], n_tiles, body, 0)

    @pl.when(i == 0)
    def _():
        zbuf[...] = jnp.zeros_like(zbuf)
        zero_rows(False)
        zero_tiles(False)

    first = jnp.where(j == 0, FRONT_PAD, 0)
    tok0 = b * t_tok + j * rows - FRONT_PAD

    def tok_body(r, c):
        a0 = (tok0 + r) * TOP_K
        for k in range(TOP_K):
            _row_copy(x_ref, r, xs_hbm, dest_ref[a0 + k], sem_tok).start()
        return c
    lax.fori_loop(first, rows, tok_body, 0)

    @pl.when(j == 0)
    def _():
        def wait_body(r, c):
            for k in range(TOP_K):
                _row_copy(x_ref, 0, xs_hbm, 0, sem_tok).wait()
            return c
        lax.fori_loop(FRONT_PAD, rows, wait_body, 0)

    @pl.when(j != 0)
    def _():
        for k in range(TOP_K):
            _row_copy(x_ref, 0, xs_hbm, 0, sem_tok, rows=rows).wait()

    @pl.when(i == 0)
    def _():
        zero_rows(True)
        zero_tiles(True)


def _dispatch(dest, zlo, zhi, n_used, u2p, bsz, n, t_tok, n_tiles):
    half = u2p.shape[1]
    tm = MOE_TILE
    rows = CHUNK
    nblk = n // rows
    kern = functools.partial(_dispatch_kernel, nblk=nblk, t_tok=t_tok, n_exp=zlo.shape[0],
                             n_tiles=n_tiles, tm=tm)
    grid_spec = pltpu.PrefetchScalarGridSpec(
        num_scalar_prefetch=4,
        grid=(bsz * nblk,),
        in_specs=[pl.BlockSpec((rows, half), lambda i, de, lo, hi, nu: (i, 0))],
        out_specs=pl.BlockSpec(memory_space=pl.ANY),
        scratch_shapes=[pltpu.VMEM((tm, half), U32), pltpu.SemaphoreType.DMA(()), pltpu.SemaphoreType.DMA(())],
    )
    return pl.pallas_call(
        kern,
        grid_spec=grid_spec,
        out_shape=jax.ShapeDtypeStruct((n_tiles * tm, half), U32),
        compiler_params=_params("arbitrary"),
        name="moe_dispatch",
    )(dest, zlo, zhi, n_used, u2p)


def _final_kernel(dest_ref, h1_ref, g_ref, gain_ref, y_hbm, out_ref, ybuf_a, ybuf_b, sem, *, tm, n_tiles,
                  tn_dn):
    i = pl.program_id(0)
    group = 32
    hw = tn_dn // 2

    def issue_group(tile, buf, sem_, r0):
        base = tile * (tm * TOP_K)
        for rr in range(group):
            for k in range(TOP_K):
                pltpu.make_async_copy(y_hbm.at[pl.ds(dest_ref[base + (r0 + rr) * TOP_K + k], 1)],
                                      buf.at[k, pl.ds(r0 + rr, 1)], sem_).start()

    def wait_buf(buf, sem_):
        for k in range(TOP_K):
            pltpu.make_async_copy(y_hbm.at[pl.ds(0, tm)], buf.at[k], sem_).wait()

    @pl.when(i == 0)
    def _():
        def first(gi, c):
            issue_group(0, ybuf_a, sem.at[0], gi * group)
            return c
        lax.fori_loop(0, tm // group, first, 0)

    def step(cur, sem_cur, nxt, sem_nxt):
        wait_buf(cur, sem_cur)
        nxt_tile = jnp.minimum(i + 1, n_tiles - 1)
        gain = gain_ref[...]

        def body(gi, c):
            r0 = pl.multiple_of(gi * group, group)
            issue_group(nxt_tile, nxt, sem_nxt, r0)
            rows = pl.ds(r0, group)
            h = h1_ref[rows, :]
            g = g_ref[rows, :]
            for k in range(TOP_K):
                w = cur[k, rows, :]
                parts = []
                for jb in range(w.shape[1] // hw):
                    lo, hi = _unpack_bf16_pair(w[:, jb * hw:(jb + 1) * hw])
                    parts += [lo, hi]
                h = h + g[:, k:k + 1] * jnp.concatenate(parts, axis=1)
            out_ref[rows, :] = h * lax.rsqrt(jnp.mean(h * h, axis=-1, keepdims=True) + NORM_EPS) * gain
            return c
        lax.fori_loop(0, tm // group, body, 0)

        @pl.when(i == n_tiles - 1)
        def _():
            wait_buf(nxt, sem_nxt)

    @pl.when(i % 2 == 0)
    def _():
        step(ybuf_a, sem.at[0], ybuf_b, sem.at[1])

    @pl.when(i % 2 == 1)
    def _():
        step(ybuf_b, sem.at[1], ybuf_a, sem.at[0])


def _final(dest_seq, h1_3, gates_seq, gain, y_sorted, seq, tn_dn):
    b, n, d = h1_3.shape
    tm = CHUNK
    skip = (n - seq) // tm
    nblk = seq // tm
    n_tiles = b * nblk
    kern = functools.partial(_final_kernel, tm=tm, n_tiles=n_tiles, tn_dn=tn_dn)
    grid_spec = pltpu.PrefetchScalarGridSpec(
        num_scalar_prefetch=1,
        grid=(n_tiles,),
        in_specs=[
            pl.BlockSpec((None, tm, d), lambda i, ds: (i // nblk, skip + i % nblk, 0)),
            pl.BlockSpec((tm, TOP_K), lambda i, ds: (i, 0)),
            pl.BlockSpec((1, d), lambda i, ds: (0, 0)),
            pl.BlockSpec(memory_space=pl.ANY),
        ],
        out_specs=pl.BlockSpec((None, tm, d), lambda i, ds: (i // nblk, i % nblk, 0)),
        scratch_shapes=[pltpu.VMEM((TOP_K, tm, d // 2), U32), pltpu.VMEM((TOP_K, tm, d // 2), U32),
                        pltpu.SemaphoreType.DMA((2,))],
    )
    return pl.pallas_call(
        kern,
        grid_spec=grid_spec,
        out_shape=jax.ShapeDtypeStruct((b, seq, d), F32),
        compiler_params=_params("arbitrary"),
        name="final_norm",
    )(dest_seq, h1_3, gates_seq, gain, y_sorted)


def kernel(x, meta_tokens, norm_mix, w_in, b_mlstm_gates, mlstm_head_norm, hgrn_lower_bound,
           hgrn_head_norm, w_out, norm_ffn, w_router, b_router, w_gate_up, b_gate_up, w_down,
           b_down, norm_final):
    bsz, seq, d = x.shape
    depth = w_in.shape[0]
    assert depth == 1 and seq % CHUNK == 0
    mh = b_mlstm_gates.shape[-1] // 2
    mw = mh * MLSTM_DV
    hw = hgrn_lower_bound.shape[-1]
    hh = hw // HGRN_DK
    n_exp = w_router.shape[-1]
    n = CHUNK + seq
    t_tok = NUM_META + seq

    meta = jnp.broadcast_to(meta_tokens[None].astype(x.dtype), (bsz, NUM_META, d))
    hp = jnp.concatenate([jnp.zeros((bsz, FRONT_PAD, d), x.dtype), meta, x], axis=1)
    hp2 = hp.reshape(bsz * n, d)

    g0 = 2 * mh * MLSTM_DQK + 2 * mw
    w_in0 = w_in[0]
    w_main = jnp.concatenate([w_in0[:, :g0], w_in0[:, g0 + 2 * mh:]], axis=1).astype(BF16)
    w_gates = jnp.pad(w_in0[:, g0:g0 + 2 * mh], ((0, 0), (0, GATE_LANES - 2 * mh))).astype(BF16)
    bias_col = jnp.pad(b_mlstm_gates[0].astype(F32), (0, GATE_LANES - 2 * mh)).reshape(1, GATE_LANES)
    bias_row = b_mlstm_gates[0].astype(F32).reshape(2 * mh, 1)
    lbs = jnp.cumsum(jax.nn.softmax(hgrn_lower_bound.astype(F32), axis=0), axis=0)[0].reshape(1, hw)

    z, gates = _in_proj(hp2, norm_mix[0].reshape(1, d), w_main, w_gates)
    z3 = z.reshape(bsz, n, -1)
    gates_col = gates.reshape(bsz, n, GATE_LANES)
    gates_row = jnp.swapaxes(gates_col[:, :, :2 * mh], 1, 2)
    hm = _mlstm(z3, gates_col, gates_row, bias_col, bias_row, mlstm_head_norm[0].reshape(1, mw), mh)
    oh = _hgrn(z3, lbs, hgrn_head_norm[0].reshape(1, hw), g0, hh)

    wr = jnp.pad(w_router[0].astype(F32), ((0, 0), (0, LANES - n_exp)))
    br = jnp.pad(b_router[0].astype(F32), (0, LANES - n_exp)).reshape(1, LANES)
    wr_hi = wr.astype(BF16)
    wr_lo = (wr - wr_hi.astype(F32)).astype(BF16)
    h1, u2, logits = _out_proj(hm.reshape(bsz * n, mw), oh.reshape(bsz * n, hw), w_out[0].astype(BF16),
                               hp2, norm_ffn[0].reshape(1, d), wr_hi, wr_lo, br)

    t_all = bsz * t_tok
    tk = t_all * TOP_K
    logits_tok = logits.reshape(bsz, n, LANES)[:, FRONT_PAD:, :n_exp].reshape(t_all, n_exp)
    top_val, top_idx = lax.top_k(logits_tok, TOP_K)
    gate_w = jax.nn.softmax(top_val, axis=-1)
    flat_e = top_idx.reshape(-1).astype(jnp.int32)
    onehot = (flat_e[:, None] == jnp.arange(n_exp, dtype=jnp.int32)[None, :]).astype(jnp.int32)
    csum = jnp.cumsum(onehot, axis=0)
    rank = jnp.sum((csum - onehot) * onehot, axis=1)
    counts = csum[-1]
    tm = MOE_TILE
    padded = (counts + tm - 1) // tm * tm
    pad_end = jnp.cumsum(padded)
    pad_start = pad_end - padded
    dest = (jnp.sum(onehot * pad_start[None, :], axis=1) + rank).astype(jnp.int32)
    n_tiles = -(-(tk + n_exp * (tm - 1)) // tm)
    tile_row0 = jnp.arange(n_tiles, dtype=jnp.int32) * tm
    tile_e = jnp.minimum(jnp.sum((tile_row0[:, None] >= pad_end[None, :]).astype(jnp.int32), axis=1),
                         n_exp - 1).astype(jnp.int32)
    n_used = (pad_end[-1] // tm).astype(jnp.int32)

    ff = w_down.shape[2]
    tn_up = _largest_tile(ff, 512)
    tn_dn = _largest_tile(d, 2048)
    tile_start = (pad_start // tm).astype(jnp.int32)
    tile_count = (padded // tm).astype(jnp.int32)
    sched_up = _moe_schedule(tile_e, tile_start, tile_count, n_used, ff // tn_up, n_tiles)
    sched_dn = _moe_schedule(tile_e, tile_start, tile_count, n_used, d // tn_dn, n_tiles)

    x_sorted = _dispatch(dest, (pad_start + counts).astype(jnp.int32), pad_end.astype(jnp.int32),
                         n_used.reshape(1), u2, bsz, n, t_tok, n_tiles)
    act = _moe_up(sched_up, x_sorted, w_gate_up[0], b_gate_up[0].astype(F32).reshape(n_exp, 1, -1), tn_up)
    y_sorted = _moe_down(sched_dn, act, w_down[0], b_down[0].astype(F32).reshape(n_exp, 1, d), tn_dn)

    dest_seq = dest.reshape(bsz, t_tok, TOP_K)[:, NUM_META:, :].reshape(-1)
    gates_seq = gate_w.reshape(bsz, t_tok, TOP_K)[:, NUM_META:, :].reshape(bsz * seq, TOP_K)
    return _final(dest_seq, h1.reshape(bsz, n, d), gates_seq, norm_final.reshape(1, d), y_sorted, seq, tn_dn)
```

```python
import functools

import jax
import jax.numpy as jnp
from jax import lax
from jax.experimental import pallas as pl
from jax.experimental.pallas import tpu as pltpu

F32 = jnp.float32
BF16 = jnp.bfloat16
U32 = jnp.uint32

NUM_META = 16
MLSTM_DV = 512
MLSTM_DQK = 256
HGRN_DK = 128
GATE_SOFTCAP = 15.0
TOP_K = 4
SWIGLU_LIMIT = 7.0
SWIGLU_ALPHA = 1.702
NORM_EPS = 1e-5
HEAD_NORM_EPS = 1e-6
LOG2_E = 1.4426950408889634

LANES = 128
VMEM_LIMIT_BYTES = 58 * 1024 * 1024

CHUNK = 256
FRONT_PAD = CHUNK - NUM_META
GATE_LANES = LANES

MOE_TILE = 512

def _largest_tile(total, target):
    best = LANES
    t = LANES
    while t <= min(total, target):
        if total % t == 0:
            best = t
        t += LANES
    assert total % best == 0, (total, target)
    return best


def _log_sigmoid(x):
    return jnp.minimum(x, 0.0) - jnp.log1p(jnp.exp(-jnp.abs(x)))


def _sigmoid(x):
    return 1.0 / (1.0 + jnp.exp(-x))


def _pack_bf16_pair(lo, hi):
    lo_bits = lax.bitcast_convert_type(lo.astype(BF16).astype(F32), U32) >> 16
    hi_bits = lax.bitcast_convert_type(hi.astype(BF16).astype(F32), U32) & jnp.uint32(0xFFFF0000)
    return lo_bits | hi_bits


def _unpack_bf16_pair(w):
    lo = lax.bitcast_convert_type(w << 16, F32)
    hi = lax.bitcast_convert_type(w & jnp.uint32(0xFFFF0000), F32)
    return lo, hi


def _dot(a, b):
    return jnp.dot(a, b, preferred_element_type=F32)


def _dot_nt(a, b):
    return lax.dot_general(a, b, (((1,), (1,)), ((), ())), preferred_element_type=F32)


def _dot_tn(a, b):
    return lax.dot_general(a, b, (((0,), (0,)), ((), ())), preferred_element_type=F32)


def _params(*sem):
    return pltpu.CompilerParams(dimension_semantics=sem, vmem_limit_bytes=VMEM_LIMIT_BYTES)


def _padded_half_block(i, half, nblk, x_ref, meta_ref):
    is_head = ((2 * i + half) % nblk) == 0
    head = jnp.concatenate([jnp.zeros((FRONT_PAD, meta_ref.shape[1]), F32), meta_ref[...]], axis=0)
    return jnp.where(is_head, head, x_ref[...])


def _half_block_specs(d, nblk):
    def x_map(half):
        def index_map(i, j):
            hb = 2 * i + half
            return (hb // nblk, jnp.maximum(hb % nblk - 1, 0), 0)
        return index_map
    return [pl.BlockSpec((None, CHUNK, d), x_map(0), pipeline_mode=pl.Buffered(1)),
            pl.BlockSpec((None, CHUNK, d), x_map(1), pipeline_mode=pl.Buffered(1)),
            pl.BlockSpec((NUM_META, d), lambda i, j: (0, 0))]


def _inproj_kernel(xa_ref, xb_ref, meta_ref, gain_ref, wg_ref, w_ref, z_ref, g_ref, u_scr, *, nblk):
    @pl.when(pl.program_id(1) == 0)
    def _():
        for half, x_ref in enumerate((xa_ref, xb_ref)):
            x = _padded_half_block(pl.program_id(0), half, nblk, x_ref, meta_ref)
            ms = jnp.mean(x * x, axis=-1, keepdims=True)
            u = (x * lax.rsqrt(ms + NORM_EPS) * gain_ref[...]).astype(BF16)
            rows = slice(half * CHUNK, (half + 1) * CHUNK)
            u_scr[rows, :] = u
            g_ref[rows, :] = _dot(u, wg_ref[...])

    z_ref[...] = _dot(u_scr[...], w_ref[...]).astype(z_ref.dtype)


def _in_proj(x, meta, gain, w_main, w_gates):
    bsz, seq, d = x.shape
    nblk = 1 + seq // CHUNK
    m = bsz * nblk * CHUNK
    zw = w_main.shape[1]
    tm = 2 * CHUNK
    assert m % tm == 0
    tn = _largest_tile(zw, 2048)
    return pl.pallas_call(
        functools.partial(_inproj_kernel, nblk=nblk),
        grid=(m // tm, zw // tn),
        in_specs=_half_block_specs(d, nblk) + [
            pl.BlockSpec((1, d), lambda i, j: (0, 0)),
            pl.BlockSpec((d, GATE_LANES), lambda i, j: (0, 0)),
            pl.BlockSpec((d, tn), lambda i, j: (0, j)),
        ],
        out_specs=[
            pl.BlockSpec((tm, tn), lambda i, j: (i, j)),
            pl.BlockSpec((tm, GATE_LANES), lambda i, j: (i, 0)),
        ],
        out_shape=[
            jax.ShapeDtypeStruct((m, zw), BF16),
            jax.ShapeDtypeStruct((m, GATE_LANES), F32),
        ],
        scratch_shapes=[pltpu.VMEM((tm, d), BF16)],
        compiler_params=_params("parallel", "arbitrary"),
        name="in_proj",
    )(x, x, meta, gain, w_gates, w_main)


def _mlstm_kernel(q_ref, k_ref, v_ref, o_ref, gc_ref, gr_ref, bc_ref, br_ref, gain_ref, out_ref,
                  ct, nst, mst, *, heads, chunk):
    hd = pl.program_id(1)
    c = pl.program_id(2)
    L = chunk
    neg_inf = -jnp.inf

    @pl.when(c == 0)
    def _():
        ct[...] = jnp.zeros_like(ct)
        nst[...] = jnp.zeros_like(nst)
        mst[...] = jnp.zeros_like(mst)

    pos_col = lax.broadcasted_iota(jnp.int32, (L, 1), 0) + c * L
    pos_row = lax.broadcasted_iota(jnp.int32, (1, L), 1) + c * L
    pad_col = pos_col < FRONT_PAD
    pad_row = pos_row < FRONT_PAD

    gcol = gc_ref[...] + bc_ref[...]
    gcol = GATE_SOFTCAP * jnp.tanh(gcol / GATE_SOFTCAP)
    lane = lax.broadcasted_iota(jnp.int32, gcol.shape, 1)
    lf_col_all = jnp.where(pad_col, 0.0, _log_sigmoid(gcol))
    li_col = jnp.sum(jnp.where(lane == hd, gcol, 0.0), axis=1, keepdims=True)
    li_col = jnp.where(pad_col, neg_inf, li_col)

    grow = gr_ref[...] + br_ref[...]
    grow = GATE_SOFTCAP * jnp.tanh(grow / GATE_SOFTCAP)
    sub = lax.broadcasted_iota(jnp.int32, grow.shape, 0)
    lf_row_all = jnp.where(pad_row, 0.0, _log_sigmoid(grow))
    li_row = jnp.sum(jnp.where(sub == hd, grow, 0.0), axis=0, keepdims=True)
    li_row = jnp.where(pad_row, neg_inf, li_row)

    r_i = lax.broadcasted_iota(jnp.int32, (L, L), 0)
    c_i = lax.broadcasted_iota(jnp.int32, (L, L), 1)
    causal = r_i >= c_i
    lower = causal.astype(BF16)
    upper = (r_i <= c_i).astype(BF16)
    c_hi = lf_col_all.astype(BF16)
    c_lo = (lf_col_all - c_hi.astype(F32)).astype(BF16)
    a_col_all = _dot(lower, c_hi) + _dot(lower, c_lo)
    a_col = jnp.sum(jnp.where(lane == heads + hd, a_col_all, 0.0), axis=1, keepdims=True)
    r_hi = lf_row_all.astype(BF16)
    r_lo = (lf_row_all - r_hi.astype(F32)).astype(BF16)
    a_row_all = _dot(r_hi, upper) + _dot(r_lo, upper)
    a_row = jnp.sum(jnp.where(sub == heads + hd, a_row_all, 0.0), axis=0, keepdims=True)

    m_prev = mst[...]
    inter = a_col + m_prev
    d_log = jnp.where(causal, a_col - a_row + li_row, neg_inf)
    m_row = jnp.maximum(inter, jnp.max(d_log, axis=1, keepdims=True))

    q = q_ref[...]
    kf = k_ref[...].astype(F32) * (MLSTM_DQK ** -0.5)
    kb = kf.astype(BF16)
    v = v_ref[...]
    scores = _dot_nt(q, kb) * jnp.exp(d_log - m_row)
    inter_w = jnp.exp(inter - m_row)
    num = _dot(scores.astype(BF16), v) + inter_w * _dot_nt(q, ct[...].astype(BF16))
    den = jnp.sum(scores, axis=1, keepdims=True) + inter_w * jnp.sum(
        q.astype(F32) * nst[...], axis=1, keepdims=True)
    h_out = num / jnp.maximum(jnp.abs(den), jnp.exp(-m_row))

    y = h_out * lax.rsqrt(jnp.mean(h_out * h_out, axis=-1, keepdims=True) + HEAD_NORM_EPS)
    y = y * gain_ref[...] * _sigmoid(o_ref[...].astype(F32))
    out_ref[...] = y.astype(out_ref.dtype)

    a_end = a_col[L - 1:L, :]
    dec_col = a_end - a_col + li_col
    dec_row = a_end - a_row + li_row
    m_new = jnp.maximum(a_end + m_prev, jnp.max(dec_row, axis=1, keepdims=True))
    w_col = jnp.exp(dec_col - m_new)
    carry_w = jnp.exp(a_end + m_prev - m_new)
    kw = kf * w_col
    ct[...] = carry_w * ct[...] + _dot_tn(v, kw.astype(BF16))
    nst[...] = carry_w * nst[...] + jnp.sum(kw, axis=0, keepdims=True)
    mst[...] = m_new


def _mlstm(z3, gates_col, gates_row, bias_col, bias_row, gain, heads):
    b, n, _ = z3.shape
    L = CHUNK
    nc = n // L
    dq, dv = MLSTM_DQK, MLSTM_DV
    assert dv == 2 * dq
    kern = functools.partial(_mlstm_kernel, heads=heads, chunk=L)
    return pl.pallas_call(
        kern,
        grid=(b, heads, nc),
        in_specs=[
            pl.BlockSpec((None, L, dq), lambda i, h, c: (i, c, h)),
            pl.BlockSpec((None, L, dq), lambda i, h, c: (i, c, heads + h)),
            pl.BlockSpec((None, L, dv), lambda i, h, c: (i, c, heads + h)),
            pl.BlockSpec((None, L, dv), lambda i, h, c: (i, c, 2 * heads + h)),
            pl.BlockSpec((None, L, GATE_LANES), lambda i, h, c: (i, c, 0)),
            pl.BlockSpec((None, 2 * heads, L), lambda i, h, c: (i, 0, c)),
            pl.BlockSpec((1, GATE_LANES), lambda i, h, c: (0, 0)),
            pl.BlockSpec((2 * heads, 1), lambda i, h, c: (0, 0)),
            pl.BlockSpec((1, dv), lambda i, h, c: (0, h)),
        ],
        out_specs=pl.BlockSpec((None, L, dv), lambda i, h, c: (i, c, h)),
        out_shape=jax.ShapeDtypeStruct((b, n, heads * dv), BF16),
        scratch_shapes=[pltpu.VMEM((dv, dq), F32), pltpu.VMEM((1, dq), F32), pltpu.VMEM((1, 1), F32)],
        compiler_params=_params("parallel", "parallel", "arbitrary"),
        name="mlstm",
    )(z3, z3, z3, z3, gates_col, gates_row, bias_col, bias_row, gain)


def _block_ref_rows(g, hs, L):
    d = g.shape[1]
    if hs >= 4:
        blk = 2 * hs
        g3 = g.reshape(L // blk, blk, d)
        return jnp.broadcast_to(g3[:, hs - 1:hs, :], g3.shape).reshape(L, d)
    row = lax.broadcasted_iota(jnp.int32, (L, 1), 0)
    if hs == 2:
        r = row & 3
        up1 = pltpu.roll(g, L - 1, 0)
        dn1 = pltpu.roll(g, 1, 0)
        dn2 = pltpu.roll(g, 2, 0)
        return jnp.where(r == 0, up1, jnp.where(r == 1, g, jnp.where(r == 2, dn1, dn2)))
    assert hs == 1
    return jnp.where((row & 1) == 1, pltpu.roll(g, 1, 0), g)


def _hgrn_kernel(q_ref, f_ref, i_ref, g_ref, lb_ref, gain_ref, tri_ref, lvl_ref, out_ref, st,
                 *, chunk, heads_per_step):
    c = pl.program_id(2)
    L = chunk
    dk = HGRN_DK

    @pl.when(c == 0)
    def _():
        st[...] = jnp.zeros_like(st)

    row = lax.broadcasted_iota(jnp.int32, (L, 1), 0)
    is_pad = (row + c * L) < FRONT_PAD
    tri = tri_ref[...]

    for h in range(heads_per_step):
        cols = slice(h * dk, (h + 1) * dk)
        lb = lb_ref[:, cols]
        f_pre = f_ref[:, cols].astype(F32)
        t0 = jnp.log(lb)
        t1 = jnp.log1p(-lb) + _log_sigmoid(f_pre)
        log_f = jnp.maximum(t0, t1) + jnp.log1p(jnp.exp(-jnp.abs(t0 - t1)))
        kh = (1.0 - lb) * _sigmoid(-f_pre)
        log_f = jnp.where(is_pad, 0.0, log_f)
        kh = jnp.where(is_pad, 0.0, kh)
        qf = q_ref[:, cols].astype(F32)
        qh = qf * _sigmoid(qf) * (HGRN_DK ** -0.5)
        v = i_ref[:, cols]

        f_hi = log_f.astype(BF16)
        f_lo = (log_f - f_hi.astype(F32)).astype(BF16)
        g2 = _dot(tri, jnp.concatenate([f_hi, f_lo], axis=1))
        g_cum = g2[:, :dk] + g2[:, dk:]
        g_log2 = g_cum * LOG2_E

        o = _dot_nt((qh * jnp.exp2(g_log2)).astype(BF16), st[h].astype(BF16))

        diag = jnp.sum(qh * kh, axis=1, keepdims=True)
        attn = jnp.where(lvl_ref[...] == 0, diag, 0.0)
        hs = 1
        while hs < L:
            dj = g_log2 - _block_ref_rows(g_log2, hs, L)
            second = (row & hs) != 0
            qj = (qh * jnp.exp2(jnp.where(second, dj, -jnp.inf))).astype(BF16)
            kj = (kh * jnp.exp2(jnp.where(second, -jnp.inf, -dj))).astype(BF16)
            attn = jnp.where(lvl_ref[...] >= hs, _dot_nt(qj, kj), attn)
            hs *= 2
        o = o + _dot(attn.astype(BF16), v)

        y = o * lax.rsqrt(jnp.mean(o * o, axis=-1, keepdims=True) + HEAD_NORM_EPS)
        gate = g_ref[:, cols].astype(F32)
        y = y * gain_ref[:, cols] * (gate * _sigmoid(gate))
        out_ref[:, cols] = y.astype(out_ref.dtype)

        g_end = g_log2[L - 1:L, :]
        k_dec = (kh * jnp.exp2(g_end - g_log2)).astype(BF16)
        st[h] = jnp.exp2(g_end) * st[h] + _dot_tn(v, k_dec)


def _hgrn(z3, lb, gain, col0, heads):
    b, n, _ = z3.shape
    L = CHUNK
    nc = n // L
    dk = HGRN_DK
    hb = 1
    for cand in (2, 4):
        if heads % cand == 0 and (col0 // dk) % cand == 0:
            hb = cand
    wb = hb * dk
    base = col0 // wb
    kern = functools.partial(_hgrn_kernel, chunk=L, heads_per_step=hb)
    r_i = lax.broadcasted_iota(jnp.int32, (L, L), 0)
    c_i = lax.broadcasted_iota(jnp.int32, (L, L), 1)
    tri = (r_i >= c_i).astype(BF16)
    lvl = r_i ^ c_i

    def zspec(k):
        return pl.BlockSpec((None, L, wb), lambda i, h, c: (i, c, base + k * (heads // hb) + h))

    return pl.pallas_call(
        kern,
        grid=(b, heads // hb, nc),
        in_specs=[zspec(0), zspec(1), zspec(2), zspec(3),
                  pl.BlockSpec((1, wb), lambda i, h, c: (0, h)),
                  pl.BlockSpec((1, wb), lambda i, h, c: (0, h)),
                  pl.BlockSpec((L, L), lambda i, h, c: (0, 0)),
                  pl.BlockSpec((L, L), lambda i, h, c: (0, 0))],
        out_specs=pl.BlockSpec((None, L, wb), lambda i, h, c: (i, c, h)),
        out_shape=jax.ShapeDtypeStruct((b, n, heads * dk), BF16),
        scratch_shapes=[pltpu.VMEM((hb, dk, dk), F32)],
        compiler_params=_params("parallel", "parallel", "arbitrary"),
        name="hgrn2",
    )(z3, z3, z3, z3, lb, gain, tri, lvl)


def _outproj_kernel(xa_ref, xb_ref, meta_ref, hm_ref, oh_ref, w_ref, gain_ref, wrh_ref, wrl_ref, br_ref,
                    h1_ref, u2_ref, lg_ref, *, nk, nblk):
    k = pl.program_id(1)

    @pl.when(k == 0)
    def _():
        for half, x_ref in enumerate((xa_ref, xb_ref)):
            h1_ref[half * CHUNK:(half + 1) * CHUNK, :] = _padded_half_block(
                pl.program_id(0), half, nblk, x_ref, meta_ref)

    @pl.when(k < nk)
    def _():
        h1_ref[...] += _dot(hm_ref[...], w_ref[...])

    @pl.when(k >= nk)
    def _():
        h1_ref[...] += _dot(oh_ref[...], w_ref[...])

    @pl.when(k == pl.num_programs(1) - 1)
    def _():
        h1 = h1_ref[...]
        u2 = h1 * lax.rsqrt(jnp.mean(h1 * h1, axis=-1, keepdims=True) + NORM_EPS) * gain_ref[...]
        half = u2.shape[1] // 2
        u2_ref[...] = _pack_bf16_pair(u2[:, :half], u2[:, half:])
        u_hi = u2.astype(BF16)
        u_lo = (u2 - u_hi.astype(F32)).astype(BF16)
        lg_ref[...] = (_dot(u_hi, wrh_ref[...]) + _dot(u_lo, wrh_ref[...]) + _dot(u_hi, wrl_ref[...])
                       + br_ref[...])


def _out_proj(x, meta, hm, oh, w_out_b, gain, w_router_hi, w_router_lo, b_router):
    m, w1 = hm.shape
    w2 = oh.shape[1]
    d = w_out_b.shape[1]
    assert w1 == w2
    nblk = 1 + x.shape[1] // CHUNK
    tm = 2 * CHUNK
    assert m % tm == 0
    tk = _largest_tile(w1, 512)
    nk = w1 // tk
    return pl.pallas_call(
        functools.partial(_outproj_kernel, nk=nk, nblk=nblk),
        grid=(m // tm, 2 * nk),
        in_specs=_half_block_specs(d, nblk) + [
            pl.BlockSpec((tm, tk), lambda i, k: (i, jnp.minimum(k, nk - 1))),
            pl.BlockSpec((tm, tk), lambda i, k: (i, jnp.maximum(k - nk, 0))),
            pl.BlockSpec((tk, d), lambda i, k: (k, 0)),
            pl.BlockSpec((1, d), lambda i, k: (0, 0)),
            pl.BlockSpec((d, LANES), lambda i, k: (0, 0)),
            pl.BlockSpec((d, LANES), lambda i, k: (0, 0)),
            pl.BlockSpec((1, LANES), lambda i, k: (0, 0)),
        ],
        out_specs=[
            pl.BlockSpec((tm, d), lambda i, k: (i, 0)),
            pl.BlockSpec((tm, d // 2), lambda i, k: (i, 0)),
            pl.BlockSpec((tm, LANES), lambda i, k: (i, 0)),
        ],
        out_shape=[
            jax.ShapeDtypeStruct((m, d), F32),
            jax.ShapeDtypeStruct((m, d // 2), U32),
            jax.ShapeDtypeStruct((m, LANES), F32),
        ],
        compiler_params=_params("parallel", "arbitrary"),
        name="out_proj",
    )(x, x, meta, hm, oh, w_out_b, gain, w_router_hi, w_router_lo, b_router)


def _group_weight_pipeline(s, sf_ref, sg_ref, ge_ref, gj_ref, nu_ref, copies, cast):
    @pl.when(jnp.logical_and(s < nu_ref[0], sf_ref[s] == 1))
    def _():
        g = sg_ref[s]
        slot = g % 2

        @pl.when(g == 0)
        def _():
            for cp in copies(ge_ref[0], gj_ref[0], 0):
                cp.start()

        for cp in copies(ge_ref[g], gj_ref[g], slot):
            cp.wait()

        @pl.when(g + 1 < nu_ref[1])
        def _():
            for cp in copies(ge_ref[g + 1], gj_ref[g + 1], 1 - slot):
                cp.start()

        cast(slot)


def _moe_up_kernel(st_ref, se_ref, sj_ref, sf_ref, sg_ref, ge_ref, gj_ref, nu_ref, x_ref, w_hbm, bg_ref,
                   bl_ref, act_ref, wbuf, wgb, wlb, sem, *, n_up):
    s = pl.program_id(0)
    valid = s < nu_ref[0]
    tn = wgb.shape[1]

    def copies(e, j, slot):
        return [pltpu.make_async_copy(w_hbm.at[e, :, pl.ds(pl.multiple_of((part * n_up + j) * tn, tn), tn)],
                                      wbuf.at[slot, part], sem.at[slot, part]) for part in range(2)]

    def cast(slot):
        wgb[...] = wbuf[slot, 0].astype(BF16)
        wlb[...] = wbuf[slot, 1].astype(BF16)

    _group_weight_pipeline(s, sf_ref, sg_ref, ge_ref, gj_ref, nu_ref, copies, cast)

    @pl.when(valid)
    def _():
        half = wgb.shape[0] // 2
        lo, hi = _unpack_bf16_pair(x_ref[...])
        lo = lo.astype(BF16)
        hi = hi.astype(BF16)
        gate = _dot(lo, wgb[:half, :]) + _dot(hi, wgb[half:, :]) + bg_ref[...]
        lin = _dot(lo, wlb[:half, :]) + _dot(hi, wlb[half:, :]) + bl_ref[...]
        gate = jnp.minimum(gate, SWIGLU_LIMIT)
        lin = jnp.clip(lin, -SWIGLU_LIMIT, SWIGLU_LIMIT)
        act_ref[...] = (gate * _sigmoid(SWIGLU_ALPHA * gate) * (lin + 1.0)).astype(act_ref.dtype)

    @pl.when(jnp.logical_not(valid))
    def _():
        act_ref[...] = jnp.zeros_like(act_ref)


def _moe_down_kernel(st_ref, se_ref, sj_ref, sf_ref, sg_ref, ge_ref, gj_ref, nu_ref, a_ref, w_hbm, bd_ref,
                     y_ref, wbuf, wdb, sem):
    s = pl.program_id(0)
    valid = s < nu_ref[0]
    tn = wdb.shape[1]

    def copies(e, j, slot):
        return [pltpu.make_async_copy(w_hbm.at[e, :, pl.ds(pl.multiple_of(j * tn, tn), tn)],
                                      wbuf.at[slot], sem.at[slot])]

    def cast(slot):
        wdb[...] = wbuf[slot].astype(BF16)

    _group_weight_pipeline(s, sf_ref, sg_ref, ge_ref, gj_ref, nu_ref, copies, cast)

    @pl.when(valid)
    def _():
        y = _dot(a_ref[...], wdb[...]) + bd_ref[...]
        half = y.shape[1] // 2
        y_ref[...] = _pack_bf16_pair(y[:, :half], y[:, half:])

    @pl.when(jnp.logical_not(valid))
    def _():
        y_ref[...] = jnp.zeros_like(y_ref)


def _moe_schedule(tile_e, tile_start, tile_count, n_used, n_blocks, n_tiles):
    s = jnp.arange(n_blocks * n_tiles, dtype=jnp.int32)
    used = s < n_used * n_blocks
    e = jnp.repeat(tile_e, n_blocks)
    onehot = e[:, None] == jnp.arange(tile_start.shape[0], dtype=jnp.int32)[None, :]
    start_e = jnp.sum(jnp.where(onehot, tile_start[None, :], 0), axis=1)
    nt = jnp.maximum(jnp.sum(jnp.where(onehot, tile_count[None, :], 0), axis=1), 1)
    local = s - n_blocks * start_e
    sj = jnp.where(used, local // nt, s % n_blocks).astype(jnp.int32)
    st = jnp.where(used, start_e + local % nt, s // n_blocks).astype(jnp.int32)
    sf = jnp.logical_and(used, local % nt == 0).astype(jnp.int32)
    n_exp = tile_start.shape[0]
    nonempty = (tile_count > 0).astype(jnp.int32)
    ne_cum = jnp.cumsum(nonempty)
    ne_rank = ne_cum - nonempty
    sg = (jnp.sum(jnp.where(onehot, ne_rank[None, :], 0), axis=1) * n_blocks + sj).astype(jnp.int32)
    gi = jnp.arange(n_exp * n_blocks, dtype=jnp.int32)
    ge = jnp.minimum(jnp.sum((ne_cum[None, :] <= (gi // n_blocks)[:, None]).astype(jnp.int32), axis=1),
                     n_exp - 1).astype(jnp.int32)
    gj = (gi % n_blocks).astype(jnp.int32)
    nu = jnp.stack([n_used * n_blocks, ne_cum[-1] * n_blocks]).astype(jnp.int32)
    return st, e.astype(jnp.int32), sj, sf, sg, ge, gj, nu


def _clamped_step(s, nu):
    return jnp.minimum(s, nu[0] - 1)


def _moe_up(sched, x_sorted, w_gu, b_gu, tn_up):
    p, half = x_sorted.shape
    d = 2 * half
    ff = w_gu.shape[2] // 2
    tm = MOE_TILE
    n_up = ff // tn_up
    n_steps = sched[0].shape[0]
    cl = _clamped_step

    grid_spec = pltpu.PrefetchScalarGridSpec(
        num_scalar_prefetch=8,
        grid=(n_steps,),
        in_specs=[
            pl.BlockSpec((tm, half), lambda s, st, se, sj, sf, sg, ge, gj, nu: (st[cl(s, nu)], 0)),
            pl.BlockSpec(memory_space=pl.ANY),
            pl.BlockSpec((None, 1, tn_up),
                         lambda s, st, se, sj, sf, sg, ge, gj, nu: (se[cl(s, nu)], 0, sj[cl(s, nu)])),
            pl.BlockSpec((None, 1, tn_up),
                         lambda s, st, se, sj, sf, sg, ge, gj, nu: (se[cl(s, nu)], 0, n_up + sj[cl(s, nu)])),
        ],
        out_specs=pl.BlockSpec((tm, tn_up), lambda s, st, se, sj, sf, sg, ge, gj, nu: (st[s], sj[s])),
        scratch_shapes=[pltpu.VMEM((2, 2, d, tn_up), w_gu.dtype), pltpu.VMEM((d, tn_up), BF16),
                        pltpu.VMEM((d, tn_up), BF16), pltpu.SemaphoreType.DMA((2, 2))],
    )
    return pl.pallas_call(
        functools.partial(_moe_up_kernel, n_up=n_up),
        grid_spec=grid_spec,
        out_shape=jax.ShapeDtypeStruct((p, ff), BF16),
        compiler_params=_params("arbitrary"),
        name="moe_up",
    )(*sched, x_sorted, w_gu, b_gu, b_gu)


def _moe_down(sched, act, w_dn, b_dn, tn_dn):
    p, ff = act.shape
    d = w_dn.shape[2]
    tm = MOE_TILE
    n_steps = sched[0].shape[0]
    cl = _clamped_step

    grid_spec = pltpu.PrefetchScalarGridSpec(
        num_scalar_prefetch=8,
        grid=(n_steps,),
        in_specs=[
            pl.BlockSpec((tm, ff), lambda s, st, se, sj, sf, sg, ge, gj, nu: (st[cl(s, nu)], 0)),
            pl.BlockSpec(memory_space=pl.ANY),
            pl.BlockSpec((None, 1, tn_dn),
                         lambda s, st, se, sj, sf, sg, ge, gj, nu: (se[cl(s, nu)], 0, sj[cl(s, nu)])),
        ],
        out_specs=pl.BlockSpec((tm, tn_dn // 2), lambda s, st, se, sj, sf, sg, ge, gj, nu: (st[s], sj[s])),
        scratch_shapes=[pltpu.VMEM((2, ff, tn_dn), w_dn.dtype), pltpu.VMEM((ff, tn_dn), BF16),
                        pltpu.SemaphoreType.DMA((2,))],
    )
    return pl.pallas_call(
        _moe_down_kernel,
        grid_spec=grid_spec,
        out_shape=jax.ShapeDtypeStruct((p, d // 2), U32),
        compiler_params=_params("arbitrary"),
        name="moe_down",
    )(*sched, act, w_dn, b_dn)


def _row_copy(src, src_row, dst, dst_row, sem, rows=1):
    return pltpu.make_async_copy(src.at[pl.ds(src_row, rows)], dst.at[pl.ds(dst_row, rows)], sem)


def _dispatch_kernel(dest_ref, zlo_ref, zhi_ref, nused_ref, x_ref, xs_hbm, zbuf, sem_tok, sem_zero,
                     *, nblk, t_tok, n_exp, n_tiles, tm):
    i = pl.program_id(0)
    b = i // nblk
    j = i % nblk
    rows = x_ref.shape[0]

    def zero_rows(wait):
        def per_expert(e, c):
            def body(p, c2):
                cp = _row_copy(zbuf, 0, xs_hbm, p, sem_zero)
                cp.wait() if wait else cp.start()
                return c2
            return lax.fori_loop(zlo_ref[e], zhi_ref[e], body, c)
        lax.fori_loop(0, n_exp, per_expert, 0)

    def zero_tiles(wait):
        def body(t, c):
            cp = _row_copy(zbuf, 0, xs_hbm, pl.multiple_of(t * tm, tm), sem_zero, rows=tm)
            cp.wait() if wait else cp.start()
            return c
        lax.fori_loop(nused_ref[0], n_tiles, body, 0)

    @pl.when(i == 0)
    def _():
        zbuf[...] = jnp.zeros_like(zbuf)
        zero_rows(False)
        zero_tiles(False)

    first = jnp.where(j == 0, FRONT_PAD, 0)
    tok0 = b * t_tok + j * rows - FRONT_PAD

    def tok_body(r, c):
        a0 = (tok0 + r) * TOP_K
        for k in range(TOP_K):
            _row_copy(x_ref, r, xs_hbm, dest_ref[a0 + k], sem_tok).start()
        return c
    lax.fori_loop(first, rows, tok_body, 0)

    @pl.when(j == 0)
    def _():
        def wait_body(r, c):
            for k in range(TOP_K):
                _row_copy(x_ref, 0, xs_hbm, 0, sem_tok).wait()
            return c
        lax.fori_loop(FRONT_PAD, rows, wait_body, 0)

    @pl.when(j != 0)
    def _():
        for k in range(TOP_K):
            _row_copy(x_ref, 0, xs_hbm, 0, sem_tok, rows=rows).wait()

    @pl.when(i == 0)
    def _():
        zero_rows(True)
        zero_tiles(True)


def _dispatch(dest, zlo, zhi, n_used, u2p, bsz, n, t_tok, n_tiles):
    half = u2p.shape[1]
    tm = MOE_TILE
    rows = CHUNK
    nblk = n // rows
    kern = functools.partial(_dispatch_kernel, nblk=nblk, t_tok=t_tok, n_exp=zlo.shape[0],
                             n_tiles=n_tiles, tm=tm)
    grid_spec = pltpu.PrefetchScalarGridSpec(
        num_scalar_prefetch=4,
        grid=(bsz * nblk,),
        in_specs=[pl.BlockSpec((rows, half), lambda i, de, lo, hi, nu: (i, 0))],
        out_specs=pl.BlockSpec(memory_space=pl.ANY),
        scratch_shapes=[pltpu.VMEM((tm, half), U32), pltpu.SemaphoreType.DMA(()), pltpu.SemaphoreType.DMA(())],
    )
    return pl.pallas_call(
        kern,
        grid_spec=grid_spec,
        out_shape=jax.ShapeDtypeStruct((n_tiles * tm, half), U32),
        compiler_params=_params("arbitrary"),
        name="moe_dispatch",
    )(dest, zlo, zhi, n_used, u2p)


def _final_kernel(dest_ref, h1_ref, g_ref, gain_ref, y_hbm, out_ref, ybuf_a, ybuf_b, sem, *, tm, n_tiles,
                  tn_dn):
    i = pl.program_id(0)
    group = 32
    hw = tn_dn // 2

    def issue_group(tile, buf, sem_, r0):
        base = tile * (tm * TOP_K)
        for rr in range(group):
            for k in range(TOP_K):
                pltpu.make_async_copy(y_hbm.at[pl.ds(dest_ref[base + (r0 + rr) * TOP_K + k], 1)],
                                      buf.at[k, pl.ds(r0 + rr, 1)], sem_).start()

    def wait_buf(buf, sem_):
        for k in range(TOP_K):
            pltpu.make_async_copy(y_hbm.at[pl.ds(0, tm)], buf.at[k], sem_).wait()

    @pl.when(i == 0)
    def _():
        def first(gi, c):
            issue_group(0, ybuf_a, sem.at[0], gi * group)
            return c
        lax.fori_loop(0, tm // group, first, 0)

    def step(cur, sem_cur, nxt, sem_nxt):
        wait_buf(cur, sem_cur)
        nxt_tile = jnp.minimum(i + 1, n_tiles - 1)
        gain = gain_ref[...]

        def body(gi, c):
            r0 = pl.multiple_of(gi * group, group)
            issue_group(nxt_tile, nxt, sem_nxt, r0)
            rows = pl.ds(r0, group)
            h = h1_ref[rows, :]
            g = g_ref[rows, :]
            for k in range(TOP_K):
                w = cur[k, rows, :]
                parts = []
                for jb in range(w.shape[1] // hw):
                    lo, hi = _unpack_bf16_pair(w[:, jb * hw:(jb + 1) * hw])
                    parts += [lo, hi]
                h = h + g[:, k:k + 1] * jnp.concatenate(parts, axis=1)
            out_ref[rows, :] = h * lax.rsqrt(jnp.mean(h * h, axis=-1, keepdims=True) + NORM_EPS) * gain
            return c
        lax.fori_loop(0, tm // group, body, 0)

        @pl.when(i == n_tiles - 1)
        def _():
            wait_buf(nxt, sem_nxt)

    @pl.when(i % 2 == 0)
    def _():
        step(ybuf_a, sem.at[0], ybuf_b, sem.at[1])

    @pl.when(i % 2 == 1)
    def _():
        step(ybuf_b, sem.at[1], ybuf_a, sem.at[0])


def _final(dest_seq, h1_3, gates_seq, gain, y_sorted, seq, tn_dn):
    b, n, d = h1_3.shape
    tm = CHUNK
    skip = (n - seq) // tm
    nblk = seq // tm
    n_tiles = b * nblk
    kern = functools.partial(_final_kernel, tm=tm, n_tiles=n_tiles, tn_dn=tn_dn)
    grid_spec = pltpu.PrefetchScalarGridSpec(
        num_scalar_prefetch=1,
        grid=(n_tiles,),
        in_specs=[
            pl.BlockSpec((None, tm, d), lambda i, ds: (i // nblk, skip + i % nblk, 0)),
            pl.BlockSpec((tm, TOP_K), lambda i, ds: (i, 0)),
            pl.BlockSpec((1, d), lambda i, ds: (0, 0)),
            pl.BlockSpec(memory_space=pl.ANY),
        ],
        out_specs=pl.BlockSpec((None, tm, d), lambda i, ds: (i // nblk, i % nblk, 0)),
        scratch_shapes=[pltpu.VMEM((TOP_K, tm, d // 2), U32), pltpu.VMEM((TOP_K, tm, d // 2), U32),
                        pltpu.SemaphoreType.DMA((2,))],
    )
    return pl.pallas_call(
        kern,
        grid_spec=grid_spec,
        out_shape=jax.ShapeDtypeStruct((b, seq, d), F32),
        compiler_params=_params("arbitrary"),
        name="final_norm",
    )(dest_seq, h1_3, gates_seq, gain, y_sorted)


def kernel(x, meta_tokens, norm_mix, w_in, b_mlstm_gates, mlstm_head_norm, hgrn_lower_bound,
           hgrn_head_norm, w_out, norm_ffn, w_router, b_router, w_gate_up, b_gate_up, w_down,
           b_down, norm_final):
    bsz, seq, d = x.shape
    depth = w_in.shape[0]
    assert depth == 1 and seq % CHUNK == 0
    mh = b_mlstm_gates.shape[-1] // 2
    mw = mh * MLSTM_DV
    hw = hgrn_lower_bound.shape[-1]
    hh = hw // HGRN_DK
    n_exp = w_router.shape[-1]
    n = CHUNK + seq
    t_tok = NUM_META + seq

    xf = x.astype(F32)
    meta = meta_tokens.astype(F32)

    g0 = 2 * mh * MLSTM_DQK + 2 * mw
    w_in0 = w_in[0]
    w_main = jnp.concatenate([w_in0[:, :g0], w_in0[:, g0 + 2 * mh:]], axis=1).astype(BF16)
    w_gates = jnp.pad(w_in0[:, g0:g0 + 2 * mh], ((0, 0), (0, GATE_LANES - 2 * mh))).astype(BF16)
    bias_col = jnp.pad(b_mlstm_gates[0].astype(F32), (0, GATE_LANES - 2 * mh)).reshape(1, GATE_LANES)
    bias_row = b_mlstm_gates[0].astype(F32).reshape(2 * mh, 1)
    lbs = jnp.cumsum(jax.nn.softmax(hgrn_lower_bound.astype(F32), axis=0), axis=0)[0].reshape(1, hw)

    z, gates = _in_proj(xf, meta, norm_mix[0].reshape(1, d), w_main, w_gates)
    z3 = z.reshape(bsz, n, -1)
    gates_col = gates.reshape(bsz, n, GATE_LANES)
    gates_row = jnp.swapaxes(gates_col[:, :, :2 * mh], 1, 2)
    hm = _mlstm(z3, gates_col, gates_row, bias_col, bias_row, mlstm_head_norm[0].reshape(1, mw), mh)
    oh = _hgrn(z3, lbs, hgrn_head_norm[0].reshape(1, hw), g0, hh)

    wr = jnp.pad(w_router[0].astype(F32), ((0, 0), (0, LANES - n_exp)))
    br = jnp.pad(b_router[0].astype(F32), (0, LANES - n_exp)).reshape(1, LANES)
    wr_hi = wr.astype(BF16)
    wr_lo = (wr - wr_hi.astype(F32)).astype(BF16)
    h1, u2, logits = _out_proj(xf, meta, hm.reshape(bsz * n, mw), oh.reshape(bsz * n, hw),
                               w_out[0].astype(BF16), norm_ffn[0].reshape(1, d), wr_hi, wr_lo, br)

    t_all = bsz * t_tok
    tk = t_all * TOP_K
    logits_tok = logits.reshape(bsz, n, LANES)[:, FRONT_PAD:, :n_exp].reshape(t_all, n_exp)
    top_val, top_idx = lax.top_k(logits_tok, TOP_K)
    gate_w = jax.nn.softmax(top_val, axis=-1)
    flat_e = top_idx.reshape(-1).astype(jnp.int32)
    onehot = (flat_e[:, None] == jnp.arange(n_exp, dtype=jnp.int32)[None, :]).astype(jnp.int32)
    csum = jnp.cumsum(onehot, axis=0)
    rank = jnp.sum((csum - onehot) * onehot, axis=1)
    counts = csum[-1]
    tm = MOE_TILE
    padded = (counts + tm - 1) // tm * tm
    pad_end = jnp.cumsum(padded)
    pad_start = pad_end - padded
    dest = (jnp.sum(onehot * pad_start[None, :], axis=1) + rank).astype(jnp.int32)
    n_tiles = -(-(tk + n_exp * (tm - 1)) // tm)
    tile_row0 = jnp.arange(n_tiles, dtype=jnp.int32) * tm
    tile_e = jnp.minimum(jnp.sum((tile_row0[:, None] >= pad_end[None, :]).astype(jnp.int32), axis=1),
                         n_exp - 1).astype(jnp.int32)
    n_used = (pad_end[-1] // tm).astype(jnp.int32)

    ff = w_down.shape[2]
    tn_up = _largest_tile(ff, 512)
    tn_dn = _largest_tile(d, 2048)
    tile_start = (pad_start // tm).astype(jnp.int32)
    tile_count = (padded // tm).astype(jnp.int32)
    sched_up = _moe_schedule(tile_e, tile_start, tile_count, n_used, ff // tn_up, n_tiles)
    sched_dn = _moe_schedule(tile_e, tile_start, tile_count, n_used, d // tn_dn, n_tiles)

    x_sorted = _dispatch(dest, (pad_start + counts).astype(jnp.int32), pad_end.astype(jnp.int32),
                         n_used.reshape(1), u2, bsz, n, t_tok, n_tiles)
    act = _moe_up(sched_up, x_sorted, w_gate_up[0], b_gate_up[0].astype(F32).reshape(n_exp, 1, -1), tn_up)
    y_sorted = _moe_down(sched_dn, act, w_down[0], b_down[0].astype(F32).reshape(n_exp, 1, d), tn_dn)

    dest_seq = dest.reshape(bsz, t_tok, TOP_K)[:, NUM_META:, :].reshape(-1)
    gates_seq = gate_w.reshape(bsz, t_tok, TOP_K)[:, NUM_META:, :].reshape(bsz * seq, TOP_K)
    return _final(dest_seq, h1.reshape(bsz, n, d), gates_seq, norm_final.reshape(1, d), y_sorted, seq, tn_dn)
```

```python
import functools

import jax
import jax.numpy as jnp
from jax import lax
from jax.experimental import pallas as pl
from jax.experimental.pallas import tpu as pltpu

F32 = jnp.float32
BF16 = jnp.bfloat16
U32 = jnp.uint32

NUM_META = 16
MLSTM_DV = 512
MLSTM_DQK = 256
HGRN_DK = 128
GATE_SOFTCAP = 15.0
TOP_K = 4
SWIGLU_LIMIT = 7.0
SWIGLU_ALPHA = 1.702
NORM_EPS = 1e-5
HEAD_NORM_EPS = 1e-6
LOG2_E = 1.4426950408889634

LANES = 128
VMEM_LIMIT_BYTES = 58 * 1024 * 1024

CHUNK = 256
FRONT_PAD = CHUNK - NUM_META
GATE_LANES = LANES

MOE_TILE = 512

def _largest_tile(total, target):
    best = LANES
    t = LANES
    while t <= min(total, target):
        if total % t == 0:
            best = t
        t += LANES
    assert total % best == 0, (total, target)
    return best


def _log_sigmoid(x):
    return jnp.minimum(x, 0.0) - jnp.log1p(jnp.exp(-jnp.abs(x)))


def _sigmoid(x):
    return 1.0 / (1.0 + jnp.exp(-x))


def _pack_bf16_pair(lo, hi):
    lo_bits = lax.bitcast_convert_type(lo.astype(BF16).astype(F32), U32) >> 16
    hi_bits = lax.bitcast_convert_type(hi.astype(BF16).astype(F32), U32) & jnp.uint32(0xFFFF0000)
    return lo_bits | hi_bits


def _unpack_bf16_pair(w):
    lo = lax.bitcast_convert_type(w << 16, F32)
    hi = lax.bitcast_convert_type(w & jnp.uint32(0xFFFF0000), F32)
    return lo, hi


def _dot(a, b):
    return jnp.dot(a, b, preferred_element_type=F32)


def _dot_nt(a, b):
    return lax.dot_general(a, b, (((1,), (1,)), ((), ())), preferred_element_type=F32)


def _dot_tn(a, b):
    return lax.dot_general(a, b, (((0,), (0,)), ((), ())), preferred_element_type=F32)


def _params(*sem):
    return pltpu.CompilerParams(dimension_semantics=sem, vmem_limit_bytes=VMEM_LIMIT_BYTES)


def _inproj_kernel(h_ref, gain_ref, wg_ref, w_ref, z_ref, g_ref, u_scr):
    @pl.when(pl.program_id(1) == 0)
    def _():
        x = h_ref[...]
        ms = jnp.mean(x * x, axis=-1, keepdims=True)
        u = (x * lax.rsqrt(ms + NORM_EPS) * gain_ref[...]).astype(BF16)
        u_scr[...] = u
        g_ref[...] = _dot(u, wg_ref[...])

    z_ref[...] = _dot(u_scr[...], w_ref[...]).astype(z_ref.dtype)


def _in_proj(hp, gain, w_main, w_gates):
    m, d = hp.shape
    zw = w_main.shape[1]
    tm = _largest_tile(m, 512)
    tn = _largest_tile(zw, 2048)
    return pl.pallas_call(
        _inproj_kernel,
        grid=(m // tm, zw // tn),
        in_specs=[
            pl.BlockSpec((tm, d), lambda i, j: (i, 0), pipeline_mode=pl.Buffered(1)),
            pl.BlockSpec((1, d), lambda i, j: (0, 0)),
            pl.BlockSpec((d, GATE_LANES), lambda i, j: (0, 0)),
            pl.BlockSpec((d, tn), lambda i, j: (0, j)),
        ],
        out_specs=[
            pl.BlockSpec((tm, tn), lambda i, j: (i, j)),
            pl.BlockSpec((tm, GATE_LANES), lambda i, j: (i, 0)),
        ],
        out_shape=[
            jax.ShapeDtypeStruct((m, zw), BF16),
            jax.ShapeDtypeStruct((m, GATE_LANES), F32),
        ],
        scratch_shapes=[pltpu.VMEM((tm, d), BF16)],
        compiler_params=_params("parallel", "arbitrary"),
        name="in_proj",
    )(hp, gain, w_gates, w_main)


def _mlstm_kernel(q_ref, k_ref, v_ref, o_ref, gc_ref, gr_ref, bc_ref, br_ref, gain_ref, out_ref,
                  ct, nst, mst, *, heads, chunk):
    hd = pl.program_id(1)
    c = pl.program_id(2)
    L = chunk
    neg_inf = -jnp.inf

    @pl.when(c == 0)
    def _():
        ct[...] = jnp.zeros_like(ct)
        nst[...] = jnp.zeros_like(nst)
        mst[...] = jnp.zeros_like(mst)

    pos_col = lax.broadcasted_iota(jnp.int32, (L, 1), 0) + c * L
    pos_row = lax.broadcasted_iota(jnp.int32, (1, L), 1) + c * L
    pad_col = pos_col < FRONT_PAD
    pad_row = pos_row < FRONT_PAD

    gcol = gc_ref[...] + bc_ref[...]
    gcol = GATE_SOFTCAP * jnp.tanh(gcol / GATE_SOFTCAP)
    lane = lax.broadcasted_iota(jnp.int32, gcol.shape, 1)
    lf_col_all = jnp.where(pad_col, 0.0, _log_sigmoid(gcol))
    li_col = jnp.sum(jnp.where(lane == hd, gcol, 0.0), axis=1, keepdims=True)
    li_col = jnp.where(pad_col, neg_inf, li_col)

    grow = gr_ref[...] + br_ref[...]
    grow = GATE_SOFTCAP * jnp.tanh(grow / GATE_SOFTCAP)
    sub = lax.broadcasted_iota(jnp.int32, grow.shape, 0)
    lf_row_all = jnp.where(pad_row, 0.0, _log_sigmoid(grow))
    li_row = jnp.sum(jnp.where(sub == hd, grow, 0.0), axis=0, keepdims=True)
    li_row = jnp.where(pad_row, neg_inf, li_row)

    r_i = lax.broadcasted_iota(jnp.int32, (L, L), 0)
    c_i = lax.broadcasted_iota(jnp.int32, (L, L), 1)
    causal = r_i >= c_i
    lower = causal.astype(BF16)
    upper = (r_i <= c_i).astype(BF16)
    c_hi = lf_col_all.astype(BF16)
    c_lo = (lf_col_all - c_hi.astype(F32)).astype(BF16)
    a_col_all = _dot(lower, c_hi) + _dot(lower, c_lo)
    a_col = jnp.sum(jnp.where(lane == heads + hd, a_col_all, 0.0), axis=1, keepdims=True)
    r_hi = lf_row_all.astype(BF16)
    r_lo = (lf_row_all - r_hi.astype(F32)).astype(BF16)
    a_row_all = _dot(r_hi, upper) + _dot(r_lo, upper)
    a_row = jnp.sum(jnp.where(sub == heads + hd, a_row_all, 0.0), axis=0, keepdims=True)

    m_prev = mst[...]
    inter = a_col + m_prev
    d_log = jnp.where(causal, a_col - a_row + li_row, neg_inf)
    m_row = jnp.maximum(inter, jnp.max(d_log, axis=1, keepdims=True))

    q = q_ref[...]
    kf = k_ref[...].astype(F32) * (MLSTM_DQK ** -0.5)
    kb = kf.astype(BF16)
    v = v_ref[...]
    scores = _dot_nt(q, kb) * jnp.exp(d_log - m_row)
    inter_w = jnp.exp(inter - m_row)
    num = _dot(scores.astype(BF16), v) + inter_w * _dot_nt(q, ct[...].astype(BF16))
    den = jnp.sum(scores, axis=1, keepdims=True) + inter_w * jnp.sum(
        q.astype(F32) * nst[...], axis=1, keepdims=True)
    h_out = num / jnp.maximum(jnp.abs(den), jnp.exp(-m_row))

    y = h_out * lax.rsqrt(jnp.mean(h_out * h_out, axis=-1, keepdims=True) + HEAD_NORM_EPS)
    y = y * gain_ref[...] * _sigmoid(o_ref[...].astype(F32))
    out_ref[...] = y.astype(out_ref.dtype)

    a_end = a_col[L - 1:L, :]
    dec_col = a_end - a_col + li_col
    dec_row = a_end - a_row + li_row
    m_new = jnp.maximum(a_end + m_prev, jnp.max(dec_row, axis=1, keepdims=True))
    w_col = jnp.exp(dec_col - m_new)
    carry_w = jnp.exp(a_end + m_prev - m_new)
    kw = kf * w_col
    ct[...] = carry_w * ct[...] + _dot_tn(v, kw.astype(BF16))
    nst[...] = carry_w * nst[...] + jnp.sum(kw, axis=0, keepdims=True)
    mst[...] = m_new


def _mlstm(z3, gates_col, gates_row, bias_col, bias_row, gain, heads):
    b, n, _ = z3.shape
    L = CHUNK
    nc = n // L
    dq, dv = MLSTM_DQK, MLSTM_DV
    assert dv == 2 * dq
    kern = functools.partial(_mlstm_kernel, heads=heads, chunk=L)
    return pl.pallas_call(
        kern,
        grid=(b, heads, nc),
        in_specs=[
            pl.BlockSpec((None, L, dq), lambda i, h, c: (i, c, h)),
            pl.BlockSpec((None, L, dq), lambda i, h, c: (i, c, heads + h)),
            pl.BlockSpec((None, L, dv), lambda i, h, c: (i, c, heads + h)),
            pl.BlockSpec((None, L, dv), lambda i, h, c: (i, c, 2 * heads + h)),
            pl.BlockSpec((None, L, GATE_LANES), lambda i, h, c: (i, c, 0)),
            pl.BlockSpec((None, 2 * heads, L), lambda i, h, c: (i, 0, c)),
            pl.BlockSpec((1, GATE_LANES), lambda i, h, c: (0, 0)),
            pl.BlockSpec((2 * heads, 1), lambda i, h, c: (0, 0)),
            pl.BlockSpec((1, dv), lambda i, h, c: (0, h)),
        ],
        out_specs=pl.BlockSpec((None, L, dv), lambda i, h, c: (i, c, h)),
        out_shape=jax.ShapeDtypeStruct((b, n, heads * dv), BF16),
        scratch_shapes=[pltpu.VMEM((dv, dq), F32), pltpu.VMEM((1, dq), F32), pltpu.VMEM((1, 1), F32)],
        compiler_params=_params("parallel", "parallel", "arbitrary"),
        name="mlstm",
    )(z3, z3, z3, z3, gates_col, gates_row, bias_col, bias_row, gain)


def _block_ref_rows(g, hs, L):
    d = g.shape[1]
    if hs >= 4:
        blk = 2 * hs
        g3 = g.reshape(L // blk, blk, d)
        return jnp.broadcast_to(g3[:, hs - 1:hs, :], g3.shape).reshape(L, d)
    row = lax.broadcasted_iota(jnp.int32, (L, 1), 0)
    if hs == 2:
        r = row & 3
        up1 = pltpu.roll(g, L - 1, 0)
        dn1 = pltpu.roll(g, 1, 0)
        dn2 = pltpu.roll(g, 2, 0)
        return jnp.where(r == 0, up1, jnp.where(r == 1, g, jnp.where(r == 2, dn1, dn2)))
    assert hs == 1
    return jnp.where((row & 1) == 1, pltpu.roll(g, 1, 0), g)


def _hgrn_kernel(q_ref, f_ref, i_ref, g_ref, lb_ref, gain_ref, tri_ref, lvl_ref, out_ref, st,
                 *, chunk, heads_per_step):
    c = pl.program_id(2)
    L = chunk
    dk = HGRN_DK

    @pl.when(c == 0)
    def _():
        st[...] = jnp.zeros_like(st)

    row = lax.broadcasted_iota(jnp.int32, (L, 1), 0)
    is_pad = (row + c * L) < FRONT_PAD
    tri = tri_ref[...]

    for h in range(heads_per_step):
        cols = slice(h * dk, (h + 1) * dk)
        lb = lb_ref[:, cols]
        f_pre = f_ref[:, cols].astype(F32)
        t0 = jnp.log(lb)
        t1 = jnp.log1p(-lb) + _log_sigmoid(f_pre)
        log_f = jnp.maximum(t0, t1) + jnp.log1p(jnp.exp(-jnp.abs(t0 - t1)))
        kh = (1.0 - lb) * _sigmoid(-f_pre)
        log_f = jnp.where(is_pad, 0.0, log_f)
        kh = jnp.where(is_pad, 0.0, kh)
        qf = q_ref[:, cols].astype(F32)
        qh = qf * _sigmoid(qf) * (HGRN_DK ** -0.5)
        v = i_ref[:, cols]

        f_hi = log_f.astype(BF16)
        f_lo = (log_f - f_hi.astype(F32)).astype(BF16)
        g2 = _dot(tri, jnp.concatenate([f_hi, f_lo], axis=1))
        g_cum = g2[:, :dk] + g2[:, dk:]
        g_log2 = g_cum * LOG2_E

        o = _dot_nt((qh * jnp.exp2(g_log2)).astype(BF16), st[h].astype(BF16))

        diag = jnp.sum(qh * kh, axis=1, keepdims=True)
        attn = jnp.where(lvl_ref[...] == 0, diag, 0.0)
        hs = 1
        while hs < L:
            dj = g_log2 - _block_ref_rows(g_log2, hs, L)
            second = (row & hs) != 0
            qj = (qh * jnp.exp2(jnp.where(second, dj, -jnp.inf))).astype(BF16)
            kj = (kh * jnp.exp2(jnp.where(second, -jnp.inf, -dj))).astype(BF16)
            attn = jnp.where(lvl_ref[...] >= hs, _dot_nt(qj, kj), attn)
            hs *= 2
        o = o + _dot(attn.astype(BF16), v)

        y = o * lax.rsqrt(jnp.mean(o * o, axis=-1, keepdims=True) + HEAD_NORM_EPS)
        gate = g_ref[:, cols].astype(F32)
        y = y * gain_ref[:, cols] * (gate * _sigmoid(gate))
        out_ref[:, cols] = y.astype(out_ref.dtype)

        g_end = g_log2[L - 1:L, :]
        k_dec = (kh * jnp.exp2(g_end - g_log2)).astype(BF16)
        st[h] = jnp.exp2(g_end) * st[h] + _dot_tn(v, k_dec)


def _hgrn(z3, lb, gain, col0, heads):
    b, n, _ = z3.shape
    L = CHUNK
    nc = n // L
    dk = HGRN_DK
    hb = 1
    for cand in (2, 4, 8):
        if heads % cand == 0 and (col0 // dk) % cand == 0:
            hb = cand
    wb = hb * dk
    base = col0 // wb
    kern = functools.partial(_hgrn_kernel, chunk=L, heads_per_step=hb)
    r_i = lax.broadcasted_iota(jnp.int32, (L, L), 0)
    c_i = lax.broadcasted_iota(jnp.int32, (L, L), 1)
    tri = (r_i >= c_i).astype(BF16)
    lvl = r_i ^ c_i

    def zspec(k):
        return pl.BlockSpec((None, L, wb), lambda i, h, c: (i, c, base + k * (heads // hb) + h))

    return pl.pallas_call(
        kern,
        grid=(b, heads // hb, nc),
        in_specs=[zspec(0), zspec(1), zspec(2), zspec(3),
                  pl.BlockSpec((1, wb), lambda i, h, c: (0, h)),
                  pl.BlockSpec((1, wb), lambda i, h, c: (0, h)),
                  pl.BlockSpec((L, L), lambda i, h, c: (0, 0)),
                  pl.BlockSpec((L, L), lambda i, h, c: (0, 0))],
        out_specs=pl.BlockSpec((None, L, wb), lambda i, h, c: (i, c, h)),
        out_shape=jax.ShapeDtypeStruct((b, n, heads * dk), BF16),
        scratch_shapes=[pltpu.VMEM((hb, dk, dk), F32)],
        compiler_params=_params("parallel", "parallel", "arbitrary"),
        name="hgrn2",
    )(z3, z3, z3, z3, lb, gain, tri, lvl)


def _outproj_kernel(hm_ref, oh_ref, w_ref, res_ref, gain_ref, wrh_ref, wrl_ref, br_ref,
                    h1_ref, u2_ref, lg_ref, *, nk):
    k = pl.program_id(1)

    @pl.when(k == 0)
    def _():
        h1_ref[...] = res_ref[...]

    @pl.when(k < nk)
    def _():
        h1_ref[...] += _dot(hm_ref[...], w_ref[...])

    @pl.when(k >= nk)
    def _():
        h1_ref[...] += _dot(oh_ref[...], w_ref[...])

    @pl.when(k == pl.num_programs(1) - 1)
    def _():
        h1 = h1_ref[...]
        u2 = h1 * lax.rsqrt(jnp.mean(h1 * h1, axis=-1, keepdims=True) + NORM_EPS) * gain_ref[...]
        half = u2.shape[1] // 2
        u2_ref[...] = _pack_bf16_pair(u2[:, :half], u2[:, half:])
        u_hi = u2.astype(BF16)
        u_lo = (u2 - u_hi.astype(F32)).astype(BF16)
        lg_ref[...] = (_dot(u_hi, wrh_ref[...]) + _dot(u_lo, wrh_ref[...]) + _dot(u_hi, wrl_ref[...])
                       + br_ref[...])


def _out_proj(hm, oh, w_out_b, hp, gain, w_router_hi, w_router_lo, b_router):
    m, w1 = hm.shape
    w2 = oh.shape[1]
    d = w_out_b.shape[1]
    assert w1 == w2
    tm = _largest_tile(m, 384)
    tk = _largest_tile(w1, 1024)
    nk = w1 // tk
    return pl.pallas_call(
        functools.partial(_outproj_kernel, nk=nk),
        grid=(m // tm, 2 * nk),
        in_specs=[
            pl.BlockSpec((tm, tk), lambda i, k: (i, jnp.minimum(k, nk - 1))),
            pl.BlockSpec((tm, tk), lambda i, k: (i, jnp.maximum(k - nk, 0))),
            pl.BlockSpec((tk, d), lambda i, k: (k, 0)),
            pl.BlockSpec((tm, d), lambda i, k: (i, 0)),
            pl.BlockSpec((1, d), lambda i, k: (0, 0)),
            pl.BlockSpec((d, LANES), lambda i, k: (0, 0)),
            pl.BlockSpec((d, LANES), lambda i, k: (0, 0)),
            pl.BlockSpec((1, LANES), lambda i, k: (0, 0)),
        ],
        out_specs=[
            pl.BlockSpec((tm, d), lambda i, k: (i, 0)),
            pl.BlockSpec((tm, d // 2), lambda i, k: (i, 0)),
            pl.BlockSpec((tm, LANES), lambda i, k: (i, 0)),
        ],
        out_shape=[
            jax.ShapeDtypeStruct((m, d), F32),
            jax.ShapeDtypeStruct((m, d // 2), U32),
            jax.ShapeDtypeStruct((m, LANES), F32),
        ],
        compiler_params=_params("parallel", "arbitrary"),
        name="out_proj",
    )(hm, oh, w_out_b, hp, gain, w_router_hi, w_router_lo, b_router)


def _group_weight_pipeline(s, sf_ref, sg_ref, ge_ref, gj_ref, nu_ref, copies, cast):
    @pl.when(jnp.logical_and(s < nu_ref[0], sf_ref[s] == 1))
    def _():
        g = sg_ref[s]
        slot = g % 2

        @pl.when(g == 0)
        def _():
            for cp in copies(ge_ref[0], gj_ref[0], 0):
                cp.start()

        for cp in copies(ge_ref[g], gj_ref[g], slot):
            cp.wait()

        @pl.when(g + 1 < nu_ref[1])
        def _():
            for cp in copies(ge_ref[g + 1], gj_ref[g + 1], 1 - slot):
                cp.start()

        cast(slot)


def _moe_up_kernel(st_ref, se_ref, sj_ref, sf_ref, sg_ref, ge_ref, gj_ref, nu_ref, x_ref, w_hbm, bg_ref,
                   bl_ref, act_ref, wbuf, wgb, wlb, sem, *, n_up):
    s = pl.program_id(0)
    valid = s < nu_ref[0]
    tn = wgb.shape[1]

    def copies(e, j, slot):
        return [pltpu.make_async_copy(w_hbm.at[e, :, pl.ds(pl.multiple_of((part * n_up + j) * tn, tn), tn)],
                                      wbuf.at[slot, part], sem.at[slot, part]) for part in range(2)]

    def cast(slot):
        wgb[...] = wbuf[slot, 0].astype(BF16)
        wlb[...] = wbuf[slot, 1].astype(BF16)

    _group_weight_pipeline(s, sf_ref, sg_ref, ge_ref, gj_ref, nu_ref, copies, cast)

    @pl.when(valid)
    def _():
        half = wgb.shape[0] // 2
        lo, hi = _unpack_bf16_pair(x_ref[...])
        lo = lo.astype(BF16)
        hi = hi.astype(BF16)
        gate = _dot(lo, wgb[:half, :]) + _dot(hi, wgb[half:, :]) + bg_ref[...]
        lin = _dot(lo, wlb[:half, :]) + _dot(hi, wlb[half:, :]) + bl_ref[...]
        gate = jnp.minimum(gate, SWIGLU_LIMIT)
        lin = jnp.clip(lin, -SWIGLU_LIMIT, SWIGLU_LIMIT)
        act_ref[...] = (gate * _sigmoid(SWIGLU_ALPHA * gate) * (lin + 1.0)).astype(act_ref.dtype)

    @pl.when(jnp.logical_not(valid))
    def _():
        act_ref[...] = jnp.zeros_like(act_ref)


def _moe_down_kernel(st_ref, se_ref, sj_ref, sf_ref, sg_ref, ge_ref, gj_ref, nu_ref, a_ref, w_hbm, bd_ref,
                     y_ref, wbuf, wdb, sem):
    s = pl.program_id(0)
    valid = s < nu_ref[0]
    tn = wdb.shape[1]

    def copies(e, j, slot):
        return [pltpu.make_async_copy(w_hbm.at[e, :, pl.ds(pl.multiple_of(j * tn, tn), tn)],
                                      wbuf.at[slot], sem.at[slot])]

    def cast(slot):
        wdb[...] = wbuf[slot].astype(BF16)

    _group_weight_pipeline(s, sf_ref, sg_ref, ge_ref, gj_ref, nu_ref, copies, cast)

    @pl.when(valid)
    def _():
        y = _dot(a_ref[...], wdb[...]) + bd_ref[...]
        half = y.shape[1] // 2
        y_ref[...] = _pack_bf16_pair(y[:, :half], y[:, half:])

    @pl.when(jnp.logical_not(valid))
    def _():
        y_ref[...] = jnp.zeros_like(y_ref)


def _moe_schedule(tile_e, tile_start, tile_count, n_used, n_blocks, n_tiles):
    s = jnp.arange(n_blocks * n_tiles, dtype=jnp.int32)
    used = s < n_used * n_blocks
    e = jnp.repeat(tile_e, n_blocks)
    onehot = e[:, None] == jnp.arange(tile_start.shape[0], dtype=jnp.int32)[None, :]
    start_e = jnp.sum(jnp.where(onehot, tile_start[None, :], 0), axis=1)
    nt = jnp.maximum(jnp.sum(jnp.where(onehot, tile_count[None, :], 0), axis=1), 1)
    local = s - n_blocks * start_e
    sj = jnp.where(used, local // nt, s % n_blocks).astype(jnp.int32)
    st = jnp.where(used, start_e + local % nt, s // n_blocks).astype(jnp.int32)
    sf = jnp.logical_and(used, local % nt == 0).astype(jnp.int32)
    n_exp = tile_start.shape[0]
    nonempty = (tile_count > 0).astype(jnp.int32)
    ne_cum = jnp.cumsum(nonempty)
    ne_rank = ne_cum - nonempty
    sg = (jnp.sum(jnp.where(onehot, ne_rank[None, :], 0), axis=1) * n_blocks + sj).astype(jnp.int32)
    gi = jnp.arange(n_exp * n_blocks, dtype=jnp.int32)
    ge = jnp.minimum(jnp.sum((ne_cum[None, :] <= (gi // n_blocks)[:, None]).astype(jnp.int32), axis=1),
                     n_exp - 1).astype(jnp.int32)
    gj = (gi % n_blocks).astype(jnp.int32)
    nu = jnp.stack([n_used * n_blocks, ne_cum[-1] * n_blocks]).astype(jnp.int32)
    return st, e.astype(jnp.int32), sj, sf, sg, ge, gj, nu


def _clamped_step(s, nu):
    return jnp.minimum(s, nu[0] - 1)


def _moe_up(sched, x_sorted, w_gu, b_gu, tn_up):
    p, half = x_sorted.shape
    d = 2 * half
    ff = w_gu.shape[2] // 2
    tm = MOE_TILE
    n_up = ff // tn_up
    n_steps = sched[0].shape[0]
    cl = _clamped_step

    grid_spec = pltpu.PrefetchScalarGridSpec(
        num_scalar_prefetch=8,
        grid=(n_steps,),
        in_specs=[
            pl.BlockSpec((tm, half), lambda s, st, se, sj, sf, sg, ge, gj, nu: (st[cl(s, nu)], 0)),
            pl.BlockSpec(memory_space=pl.ANY),
            pl.BlockSpec((None, 1, tn_up),
                         lambda s, st, se, sj, sf, sg, ge, gj, nu: (se[cl(s, nu)], 0, sj[cl(s, nu)])),
            pl.BlockSpec((None, 1, tn_up),
                         lambda s, st, se, sj, sf, sg, ge, gj, nu: (se[cl(s, nu)], 0, n_up + sj[cl(s, nu)])),
        ],
        out_specs=pl.BlockSpec((tm, tn_up), lambda s, st, se, sj, sf, sg, ge, gj, nu: (st[s], sj[s])),
        scratch_shapes=[pltpu.VMEM((2, 2, d, tn_up), w_gu.dtype), pltpu.VMEM((d, tn_up), BF16),
                        pltpu.VMEM((d, tn_up), BF16), pltpu.SemaphoreType.DMA((2, 2))],
    )
    return pl.pallas_call(
        functools.partial(_moe_up_kernel, n_up=n_up),
        grid_spec=grid_spec,
        out_shape=jax.ShapeDtypeStruct((p, ff), BF16),
        compiler_params=_params("arbitrary"),
        name="moe_up",
    )(*sched, x_sorted, w_gu, b_gu, b_gu)


def _moe_down(sched, act, w_dn, b_dn, tn_dn):
    p, ff = act.shape
    d = w_dn.shape[2]
    tm = MOE_TILE
    n_steps = sched[0].shape[0]
    cl = _clamped_step

    grid_spec = pltpu.PrefetchScalarGridSpec(
        num_scalar_prefetch=8,
        grid=(n_steps,),
        in_specs=[
            pl.BlockSpec((tm, ff), lambda s, st, se, sj, sf, sg, ge, gj, nu: (st[cl(s, nu)], 0)),
            pl.BlockSpec(memory_space=pl.ANY),
            pl.BlockSpec((None, 1, tn_dn),
                         lambda s, st, se, sj, sf, sg, ge, gj, nu: (se[cl(s, nu)], 0, sj[cl(s, nu)])),
        ],
        out_specs=pl.BlockSpec((tm, tn_dn // 2), lambda s, st, se, sj, sf, sg, ge, gj, nu: (st[s], sj[s])),
        scratch_shapes=[pltpu.VMEM((2, ff, tn_dn), w_dn.dtype), pltpu.VMEM((ff, tn_dn), BF16),
                        pltpu.SemaphoreType.DMA((2,))],
    )
    return pl.pallas_call(
        _moe_down_kernel,
        grid_spec=grid_spec,
        out_shape=jax.ShapeDtypeStruct((p, d // 2), U32),
        compiler_params=_params("arbitrary"),
        name="moe_down",
    )(*sched, act, w_dn, b_dn)


def _row_copy(src, src_row, dst, dst_row, sem, rows=1):
    return pltpu.make_async_copy(src.at[pl.ds(src_row, rows)], dst.at[pl.ds(dst_row, rows)], sem)


def _dispatch_kernel(dest_ref, zlo_ref, zhi_ref, nused_ref, x_ref, xs_hbm, zbuf, sem_tok, sem_zero,
                     *, nblk, t_tok, n_exp, n_tiles, tm):
    i = pl.program_id(0)
    b = i // nblk
    j = i % nblk
    rows = x_ref.shape[0]

    def zero_rows(wait):
        def per_expert(e, c):
            def body(p, c2):
                cp = _row_copy(zbuf, 0, xs_hbm, p, sem_zero)
                cp.wait() if wait else cp.start()
                return c2
            return lax.fori_loop(zlo_ref[e], zhi_ref[e], body, c)
        lax.fori_loop(0, n_exp, per_expert, 0)

    def zero_tiles(wait):
        def body(t, c):
            cp = _row_copy(zbuf, 0, xs_hbm, pl.multiple_of(t * tm, tm), sem_zero, rows=tm)
            cp.wait() if wait else cp.start()
            return c
        lax.fori_loop(nused_ref[0], n_tiles, body, 0)

    @pl.when(i == 0)
    def _():
        zbuf[...] = jnp.zeros_like(zbuf)
        zero_rows(False)
        zero_tiles(False)

    first = jnp.where(j == 0, FRONT_PAD, 0)
    tok0 = b * t_tok + j * rows - FRONT_PAD

    def tok_body(r, c):
        a0 = (tok0 + r) * TOP_K
        for k in range(TOP_K):
            _row_copy(x_ref, r, xs_hbm, dest_ref[a0 + k], sem_tok).start()
        return c
    lax.fori_loop(first, rows, tok_body, 0)

    @pl.when(j == 0)
    def _():
        def wait_body(r, c):
            for k in range(TOP_K):
                _row_copy(x_ref, 0, xs_hbm, 0, sem_tok).wait()
            return c
        lax.fori_loop(FRONT_PAD, rows, wait_body, 0)

    @pl.when(j != 0)
    def _():
        for k in range(TOP_K):
            _row_copy(x_ref, 0, xs_hbm, 0, sem_tok, rows=rows).wait()

    @pl.when(i == 0)
    def _():
        zero_rows(True)
        zero_tiles(True)


def _dispatch(dest, zlo, zhi, n_used, u2p, bsz, n, t_tok, n_tiles):
    half = u2p.shape[1]
    tm = MOE_TILE
    rows = CHUNK
    nblk = n // rows
    kern = functools.partial(_dispatch_kernel, nblk=nblk, t_tok=t_tok, n_exp=zlo.shape[0],
                             n_tiles=n_tiles, tm=tm)
    grid_spec = pltpu.PrefetchScalarGridSpec(
        num_scalar_prefetch=4,
        grid=(bsz * nblk,),
        in_specs=[pl.BlockSpec((rows, half), lambda i, de, lo, hi, nu: (i, 0))],
        out_specs=pl.BlockSpec(memory_space=pl.ANY),
        scratch_shapes=[pltpu.VMEM((tm, half), U32), pltpu.SemaphoreType.DMA(()), pltpu.SemaphoreType.DMA(())],
    )
    return pl.pallas_call(
        kern,
        grid_spec=grid_spec,
        out_shape=jax.ShapeDtypeStruct((n_tiles * tm, half), U32),
        compiler_params=_params("arbitrary"),
        name="moe_dispatch",
    )(dest, zlo, zhi, n_used, u2p)


def _final_kernel(dest_ref, h1_ref, g_ref, gain_ref, y_hbm, out_ref, ybuf_a, ybuf_b, sem, *, tm, n_tiles,
                  tn_dn):
    i = pl.program_id(0)
    group = 32
    hw = tn_dn // 2

    def issue_group(tile, buf, sem_, r0):
        base = tile * (tm * TOP_K)
        for rr in range(group):
            for k in range(TOP_K):
                pltpu.make_async_copy(y_hbm.at[pl.ds(dest_ref[base + (r0 + rr) * TOP_K + k], 1)],
                                      buf.at[k, pl.ds(r0 + rr, 1)], sem_).start()

    def wait_buf(buf, sem_):
        for k in range(TOP_K):
            pltpu.make_async_copy(y_hbm.at[pl.ds(0, tm)], buf.at[k], sem_).wait()

    @pl.when(i == 0)
    def _():
        def first(gi, c):
            issue_group(0, ybuf_a, sem.at[0], gi * group)
            return c
        lax.fori_loop(0, tm // group, first, 0)

    def step(cur, sem_cur, nxt, sem_nxt):
        wait_buf(cur, sem_cur)
        nxt_tile = jnp.minimum(i + 1, n_tiles - 1)
        gain = gain_ref[...]

        def body(gi, c):
            r0 = pl.multiple_of(gi * group, group)
            issue_group(nxt_tile, nxt, sem_nxt, r0)
            rows = pl.ds(r0, group)
            h = h1_ref[rows, :]
            g = g_ref[rows, :]
            for k in range(TOP_K):
                w = cur[k, rows, :]
                parts = []
                for jb in range(w.shape[1] // hw):
                    lo, hi = _unpack_bf16_pair(w[:, jb * hw:(jb + 1) * hw])
                    parts += [lo, hi]
                h = h + g[:, k:k + 1] * jnp.concatenate(parts, axis=1)
            out_ref[rows, :] = h * lax.rsqrt(jnp.mean(h * h, axis=-1, keepdims=True) + NORM_EPS) * gain
            return c
        lax.fori_loop(0, tm // group, body, 0)

        @pl.when(i == n_tiles - 1)
        def _():
            wait_buf(nxt, sem_nxt)

    @pl.when(i % 2 == 0)
    def _():
        step(ybuf_a, sem.at[0], ybuf_b, sem.at[1])

    @pl.when(i % 2 == 1)
    def _():
        step(ybuf_b, sem.at[1], ybuf_a, sem.at[0])


def _final(dest_seq, h1_3, gates_seq, gain, y_sorted, seq, tn_dn):
    b, n, d = h1_3.shape
    tm = CHUNK
    skip = (n - seq) // tm
    nblk = seq // tm
    n_tiles = b * nblk
    kern = functools.partial(_final_kernel, tm=tm, n_tiles=n_tiles, tn_dn=tn_dn)
    grid_spec = pltpu.PrefetchScalarGridSpec(
        num_scalar_prefetch=1,
        grid=(n_tiles,),
        in_specs=[
            pl.BlockSpec((None, tm, d), lambda i, ds: (i // nblk, skip + i % nblk, 0)),
            pl.BlockSpec((tm, TOP_K), lambda i, ds: (i, 0)),
            pl.BlockSpec((1, d), lambda i, ds: (0, 0)),
            pl.BlockSpec(memory_space=pl.ANY),
        ],
        out_specs=pl.BlockSpec((None, tm, d), lambda i, ds: (i // nblk, i % nblk, 0)),
        scratch_shapes=[pltpu.VMEM((TOP_K, tm, d // 2), U32), pltpu.VMEM((TOP_K, tm, d // 2), U32),
                        pltpu.SemaphoreType.DMA((2,))],
    )
    return pl.pallas_call(
        kern,
        grid_spec=grid_spec,
        out_shape=jax.ShapeDtypeStruct((b, seq, d), F32),
        compiler_params=_params("arbitrary"),
        name="final_norm",
    )(dest_seq, h1_3, gates_seq, gain, y_sorted)


def kernel(x, meta_tokens, norm_mix, w_in, b_mlstm_gates, mlstm_head_norm, hgrn_lower_bound,
           hgrn_head_norm, w_out, norm_ffn, w_router, b_router, w_gate_up, b_gate_up, w_down,
           b_down, norm_final):
    bsz, seq, d = x.shape
    depth = w_in.shape[0]
    assert depth == 1 and seq % CHUNK == 0
    mh = b_mlstm_gates.shape[-1] // 2
    mw = mh * MLSTM_DV
    hw = hgrn_lower_bound.shape[-1]
    hh = hw // HGRN_DK
    n_exp = w_router.shape[-1]
    n = CHUNK + seq
    t_tok = NUM_META + seq

    meta = jnp.broadcast_to(meta_tokens[None].astype(x.dtype), (bsz, NUM_META, d))
    hp = jnp.concatenate([jnp.zeros((bsz, FRONT_PAD, d), x.dtype), meta, x], axis=1)
    hp2 = hp.reshape(bsz * n, d)

    g0 = 2 * mh * MLSTM_DQK + 2 * mw
    w_in0 = w_in[0]
    w_main = jnp.concatenate([w_in0[:, :g0], w_in0[:, g0 + 2 * mh:]], axis=1).astype(BF16)
    w_gates = jnp.pad(w_in0[:, g0:g0 + 2 * mh], ((0, 0), (0, GATE_LANES - 2 * mh))).astype(BF16)
    bias_col = jnp.pad(b_mlstm_gates[0].astype(F32), (0, GATE_LANES - 2 * mh)).reshape(1, GATE_LANES)
    bias_row = b_mlstm_gates[0].astype(F32).reshape(2 * mh, 1)
    lbs = jnp.cumsum(jax.nn.softmax(hgrn_lower_bound.astype(F32), axis=0), axis=0)[0].reshape(1, hw)

    z, gates = _in_proj(hp2, norm_mix[0].reshape(1, d), w_main, w_gates)
    z3 = z.reshape(bsz, n, -1)
    gates_col = gates.reshape(bsz, n, GATE_LANES)
    gates_row = jnp.swapaxes(gates_col[:, :, :2 * mh], 1, 2)
    hm = _mlstm(z3, gates_col, gates_row, bias_col, bias_row, mlstm_head_norm[0].reshape(1, mw), mh)
    oh = _hgrn(z3, lbs, hgrn_head_norm[0].reshape(1, hw), g0, hh)

    wr = jnp.pad(w_router[0].astype(F32), ((0, 0), (0, LANES - n_exp)))
    br = jnp.pad(b_router[0].astype(F32), (0, LANES - n_exp)).reshape(1, LANES)
    wr_hi = wr.astype(BF16)
    wr_lo = (wr - wr_hi.astype(F32)).astype(BF16)
    h1, u2, logits = _out_proj(hm.reshape(bsz * n, mw), oh.reshape(bsz * n, hw), w_out[0].astype(BF16),
                               hp2, norm_ffn[0].reshape(1, d), wr_hi, wr_lo, br)

    t_all = bsz * t_tok
    tk = t_all * TOP_K
    logits_tok = logits.reshape(bsz, n, LANES)[:, FRONT_PAD:, :n_exp].reshape(t_all, n_exp)
    top_val, top_idx = lax.top_k(logits_tok, TOP_K)
    gate_w = jax.nn.softmax(top_val, axis=-1)
    flat_e = top_idx.reshape(-1).astype(jnp.int32)
    onehot = (flat_e[:, None] == jnp.arange(n_exp, dtype=jnp.int32)[None, :]).astype(jnp.int32)
    csum = jnp.cumsum(onehot, axis=0)
    rank = jnp.sum((csum - onehot) * onehot, axis=1)
    counts = csum[-1]
    tm = MOE_TILE
    padded = (counts + tm - 1) // tm * tm
    pad_end = jnp.cumsum(padded)
    pad_start = pad_end - padded
    dest = (jnp.sum(onehot * pad_start[None, :], axis=1) + rank).astype(jnp.int32)
    n_tiles = -(-(tk + n_exp * (tm - 1)) // tm)
    tile_row0 = jnp.arange(n_tiles, dtype=jnp.int32) * tm
    tile_e = jnp.minimum(jnp.sum((tile_row0[:, None] >= pad_end[None, :]).astype(jnp.int32), axis=1),
                         n_exp - 1).astype(jnp.int32)
    n_used = (pad_end[-1] // tm).astype(jnp.int32)

    ff = w_down.shape[2]
    tn_up = _largest_tile(ff, 512)
    tn_dn = _largest_tile(d, 2048)
    tile_start = (pad_start // tm).astype(jnp.int32)
    tile_count = (padded // tm).astype(jnp.int32)
    sched_up = _moe_schedule(tile_e, tile_start, tile_count, n_used, ff // tn_up, n_tiles)
    sched_dn = _moe_schedule(tile_e, tile_start, tile_count, n_used, d // tn_dn, n_tiles)

    x_sorted = _dispatch(dest, (pad_start + counts).astype(jnp.int32), pad_end.astype(jnp.int32),
                         n_used.reshape(1), u2, bsz, n, t_tok, n_tiles)
    act = _moe_up(sched_up, x_sorted, w_gate_up[0], b_gate_up[0].astype(F32).reshape(n_exp, 1, -1), tn_up)
    y_sorted = _moe_down(sched_dn, act, w_down[0], b_down[0].astype(F32).reshape(n_exp, 1, d), tn_dn)

    dest_seq = dest.reshape(bsz, t_tok, TOP_K)[:, NUM_META:, :].reshape(-1)
    gates_seq = gate_w.reshape(bsz, t_tok, TOP_K)[:, NUM_META:, :].reshape(bsz * seq, TOP_K)
    return _final(dest_seq, h1.reshape(bsz, n, d), gates_seq, norm_final.reshape(1, d), y_sorted, seq, tn_dn)
```
